```python
import math
import jax, jax.numpy as jnp
from jax import lax
import numpy as np

D_MODEL = 1024
BATCH = 8
SEQ = 4096
DEPTH = 2

CHUNK = 64
D_MIX = D_MODEL
HEAD_DIM = 64
GROUP_W = D_MIX // 4
A_GROUPS = GROUP_W // HEAD_DIM
A_BLOCK = 128
B_HEADS = GROUP_W // HEAD_DIM
B_PREV_CHUNKS = 8
B_BAND = (B_PREV_CHUNKS + 1) * CHUNK
REL_CLIP = 128
C_HEADS = GROUP_W // HEAD_DIM
IDX_HEADS = 8
IDX_DIM = 64
TOPK_MAX = 256
Q_BLOCK = 128
T5_BUCKETS = 32
T5_MAX_DIST = 128
N_MEM = 256
M_HEADS = GROUP_W // HEAD_DIM
DEEPNORM_ALPHA = (2 * DEPTH) ** 0.25
DEEPNORM_BETA = (8 * DEPTH) ** -0.25
LN_EPS = 1e-5
SPLITS = (GROUP_W, GROUP_W, GROUP_W,
          GROUP_W, GROUP_W, GROUP_W, GROUP_W,
          GROUP_W, GROUP_W, GROUP_W, GROUP_W,
          IDX_HEADS * IDX_DIM, IDX_DIM, IDX_HEADS,
          GROUP_W, GROUP_W)
D_IN = sum(SPLITS)

kernel_name = "hybrid_streaming_gmlp_band_dsa_mem"


def layer_norm(x, g, b):
    xf = x.astype(jnp.float32)
    mu = jnp.mean(xf, axis=-1, keepdims=True)
    var = jnp.mean(jnp.square(xf - mu), axis=-1, keepdims=True)
    y = (xf - mu) * lax.rsqrt(var + LN_EPS)
    return (y * g.astype(jnp.float32) + b.astype(jnp.float32)).astype(x.dtype)


def softmax_f32(s, dtype):
    return jax.nn.softmax(s.astype(jnp.float32), axis=-1).astype(dtype)


def spatial_gating(u, v, ln_g, ln_b, w_s, b_s):
    bsz, seq, _ = v.shape
    nb = seq // A_BLOCK
    v = layer_norm(v, ln_g, ln_b)
    cpos = jnp.arange(A_BLOCK) // CHUNK
    mask = cpos[None, :] <= cpos[:, None]
    w = jnp.where(mask[None], w_s, 0.0).astype(v.dtype)
    vb = v.reshape(bsz, nb, A_BLOCK, A_GROUPS, HEAD_DIM)
    mixed = jnp.einsum('gij,bnjgc->bnigc', w, vb) + b_s.T.astype(v.dtype)[None, None, :, :, None]
    return u * mixed.reshape(bsz, seq, GROUP_W)


def chunk_band_attention(q, k, v, rel_bias):
    bsz, seq, h, dh = q.shape
    nc = seq // CHUNK
    qc = q.reshape(bsz, nc, CHUNK, h, dh)
    pad = ((0, 0), (B_PREV_CHUNKS, 0), (0, 0), (0, 0), (0, 0))
    kp = jnp.pad(k.reshape(bsz, nc, CHUNK, h, dh), pad)
    vp = jnp.pad(v.reshape(bsz, nc, CHUNK, h, dh), pad)
    kb = jnp.concatenate([kp[:, o:o + nc] for o in range(B_PREV_CHUNKS + 1)], axis=2)
    vb = jnp.concatenate([vp[:, o:o + nc] for o in range(B_PREV_CHUNKS + 1)], axis=2)
    s = jnp.einsum('bcqhd,bckhd->bchqk', qc, kb).astype(jnp.float32) * (dh ** -0.5)
    qi = jnp.arange(CHUNK)
    kk = jnp.arange(B_BAND)
    rel = qi[:, None] + B_PREV_CHUNKS * CHUNK - kk[None, :]
    rel_idx = jnp.clip(rel, -REL_CLIP, REL_CLIP) + REL_CLIP
    bias = rel_bias[:, rel_idx].astype(jnp.float32)
    key_chunk = jnp.arange(nc)[:, None] - B_PREV_CHUNKS + kk[None, :] // CHUNK
    valid = key_chunk >= 0
    s = jnp.where(valid[None, :, None, None, :], s + bias[None, None], -jnp.inf)
    p = softmax_f32(s, v.dtype)
    o = jnp.einsum('bchqk,bckhd->bcqhd', p, vb)
    return o.reshape(bsz, seq, h * dh)


def t5_bucket(rel):
    nb = T5_BUCKETS // 2
    max_exact = nb // 2
    ret = jnp.where(rel > 0, nb, 0)
    n = jnp.abs(rel)
    nf = jnp.maximum(n, 1).astype(jnp.float32)
    large = max_exact + (jnp.log(nf / max_exact) / math.log(T5_MAX_DIST / max_exact)
                         * (nb - max_exact)).astype(jnp.int32)
    large = jnp.minimum(large, nb - 1)
    return ret + jnp.where(n < max_exact, n, large)


def dsa_attention(q, k, v, iq, ik, iw, t5_table):
    bsz, seq, h, dh = q.shape
    k_sel = min(TOPK_MAX, seq // 4)
    nqb = seq // Q_BLOCK

    def to_blocks(a):
        return jnp.moveaxis(a.reshape((bsz, nqb, Q_BLOCK) + a.shape[2:]), 1, 0)

    key_chunk = jnp.arange(seq) // CHUNK
    ikf = ik.astype(jnp.float32)

    def block(args):
        qb, iqb, iwb, qpos = args
        qchunk = qpos // CHUNK
        logits = jnp.einsum('bqhd,bsd->bqhs', iqb.astype(jnp.float32), ikf) * (IDX_DIM ** -0.5)
        score = jnp.einsum('bqhs,bqh->bqs', jax.nn.relu(logits),
                           iwb.astype(jnp.float32) * (IDX_HEADS ** -0.5))
        admissible = key_chunk[None, :] <= qchunk[:, None]
        score = jnp.where(admissible[None], score, -jnp.inf)
        _, idx = lax.top_k(score, k_sel)
        kg = jax.vmap(lambda kk, ii: kk[ii])(k, idx)
        vg = jax.vmap(lambda vv, ii: vv[ii])(v, idx)
        s = jnp.einsum('bqhd,bqkhd->bqhk', qb, kg).astype(jnp.float32) * (dh ** -0.5)
        bias = t5_table[t5_bucket(idx - qpos[None, :, None])]
        s = s + jnp.moveaxis(bias, -1, 2).astype(jnp.float32)
        valid = (idx // CHUNK) <= qchunk[None, :, None]
        s = jnp.where(valid[:, :, None, :], s, -jnp.inf)
        p = softmax_f32(s, v.dtype)
        return jnp.einsum('bqhk,bqkhd->bqhd', p, vg)

    out = lax.map(block, (to_blocks(q), to_blocks(iq), to_blocks(iw),
                          jnp.arange(seq).reshape(nqb, Q_BLOCK)))
    return jnp.moveaxis(out, 0, 1).reshape(bsz, seq, h * dh)


def memory_attention(q, km, vm):
    bsz, seq, h, dh = q.shape
    s = jnp.einsum('bshd,bmhd->bhsm', q, km).astype(jnp.float32) * (dh ** -0.5)
    p = softmax_f32(s, vm.dtype)
    return jnp.einsum('bhsm,bmhd->bshd', p, vm).reshape(bsz, seq, h * dh)


def hybrid_layer(x, mem, w_in, b_in, a_ln_g, a_ln_b, a_ws, a_bs, b_rel, t5_table,
                 w_mem_kv, w_out, b_out, ln_g, ln_b):
    bsz, seq, _ = x.shape
    n_mem = mem.shape[1]
    proj = x @ w_in + b_in
    split_points = np.cumsum(SPLITS)[:-1].tolist()
    (a_u, a_v, a_g, bq, bk, bv, bg, cq, ck, cv, cg,
     iq, ik, iw, mq, mg) = jnp.split(proj, split_points, axis=-1)

    def heads(t):
        return t.reshape(t.shape[0], t.shape[1], -1, HEAD_DIM)

    ya = spatial_gating(jax.nn.gelu(a_u), jax.nn.gelu(a_v), a_ln_g, a_ln_b, a_ws, a_bs)
    yb = chunk_band_attention(heads(bq), heads(bk), heads(bv), b_rel)
    yc = dsa_attention(heads(cq), heads(ck), heads(cv),
                       iq.reshape(bsz, seq, IDX_HEADS, IDX_DIM), ik, iw, t5_table)
    km, vm = jnp.split(mem @ w_mem_kv, 2, axis=-1)
    ym = memory_attention(heads(mq), km.reshape(bsz, n_mem, M_HEADS, HEAD_DIM),
                          vm.reshape(bsz, n_mem, M_HEADS, HEAD_DIM))
    mixed = jnp.concatenate([ya * jax.nn.silu(a_g), yb * jax.nn.silu(bg),
                             yc * jax.nn.silu(cg), ym * jax.nn.silu(mg)], axis=-1)
    y = mixed @ w_out + b_out
    return layer_norm(DEEPNORM_ALPHA * x + y, ln_g, ln_b)


def setup_inputs(seed: int = 0) -> dict:
    key = jax.random.key(seed)
    ks = jax.random.split(key, 17)
    f32 = jnp.float32

    def nrm(k, shape, scale):
        return jax.random.normal(k, shape, f32) * scale

    return {
        "x": nrm(ks[0], (BATCH, SEQ, D_MODEL), 1.0),
        "mem": nrm(ks[1], (BATCH, N_MEM, D_MODEL), 1.0),
        "ln_in_g": 1.0 + nrm(ks[2], (D_MODEL,), 0.02),
        "ln_in_b": nrm(ks[3], (D_MODEL,), 0.02),
        "w_in": nrm(ks[4], (DEPTH, D_MODEL, D_IN), D_MODEL ** -0.5),
        "b_in": nrm(ks[5], (DEPTH, D_IN), 0.02),
        "a_ln_g": 1.0 + nrm(ks[6], (DEPTH, GROUP_W), 0.02),
        "a_ln_b": nrm(ks[7], (DEPTH, GROUP_W), 0.02),
        "a_ws": nrm(ks[8], (DEPTH, A_GROUPS, A_BLOCK, A_BLOCK), A_BLOCK ** -0.5),
        "a_bs": 1.0 + nrm(ks[9], (DEPTH, A_GROUPS, A_BLOCK), 0.02),
        "b_rel": nrm(ks[10], (DEPTH, B_HEADS, 2 * REL_CLIP + 1), 0.1),
        "t5_table": nrm(ks[11], (T5_BUCKETS, C_HEADS), 0.1),
        "w_mem_kv": nrm(ks[12], (DEPTH, D_MODEL, 2 * GROUP_W), D_MODEL ** -0.5),
        "w_out": nrm(ks[13], (DEPTH, D_MIX, D_MODEL), D_MIX ** -0.5 * DEEPNORM_BETA),
        "b_out": nrm(ks[14], (DEPTH, D_MODEL), 0.02),
        "ln_g": 1.0 + nrm(ks[15], (DEPTH, D_MODEL), 0.02),
        "ln_b": nrm(ks[16], (DEPTH, D_MODEL), 0.02),
    }


def reference(x, mem, ln_in_g, ln_in_b, w_in, b_in, a_ln_g, a_ln_b, a_ws, a_bs, b_rel,
              t5_table, w_mem_kv, w_out, b_out, ln_g, ln_b):
    h = layer_norm(x, ln_in_g, ln_in_b)
    for l in range(DEPTH):
        h = hybrid_layer(h, mem, w_in[l], b_in[l], a_ln_g[l], a_ln_b[l], a_ws[l], a_bs[l],
                         b_rel[l], t5_table, w_mem_kv[l], w_out[l], b_out[l], ln_g[l], ln_b[l])
    return h
```

```python
import functools
import math

import numpy as np
import jax
import jax.numpy as jnp
from jax import lax
from jax.experimental import pallas as pl
from jax.experimental.pallas import tpu as pltpu

F32 = jnp.float32
BF16 = jnp.bfloat16
I32 = jnp.int32

D_MODEL = 1024
CHUNK = 64
HEAD_DIM = 64
GROUP_W = 256
N_HEADS = GROUP_W // HEAD_DIM
A_BLOCK = 128
B_PREV_CHUNKS = 8
REL_CLIP = 128
IDX_HEADS = 8
IDX_DIM = 64
TOPK = 256
T5_BUCKETS = 32
T5_MAX_DIST = 128
N_MEM = 256
DEPTH = 2
DEEPNORM_ALPHA = (2 * DEPTH) ** 0.25
LN_EPS = 1e-5

TILE = 256
ROW_TILE = 512
VMEM_LIMIT = 56 * 1024 * 1024
NEG_INF = float("-inf")
INT_MIN = -2 ** 31

C_UV = (0, 512)
C_BQKV = (512, 1280)
C_CQK = (1280, 1792)
C_IQ = (1792, 2304)
C_IK = (2304, 2560)
C_IW = (2560, 2688)
C_MQ = (2688, 2944)
C_GATE = (2944, 3968)
N_PROJ = 3968


def _cparams(sem):
    return pltpu.CompilerParams(dimension_semantics=sem, vmem_limit_bytes=VMEM_LIMIT)


def _nt_dot(a, b):
    return lax.dot_general(a, b, (((1,), (1,)), ((), ())), preferred_element_type=F32)


def _gelu_tanh(x):
    return 0.5 * x * (1.0 + jnp.tanh(math.sqrt(2.0 / math.pi) * (x + 0.044715 * (x * x * x))))


def _silu(x):
    return x * (1.0 / (1.0 + jnp.exp(-x)))


def _layer_norm(x, g, b):
    mu = jnp.mean(x, axis=-1, keepdims=True)
    xc = x - mu
    var = jnp.mean(xc * xc, axis=-1, keepdims=True)
    return xc * lax.rsqrt(var + LN_EPS) * g + b


def _ln_kernel(x_ref, g_ref, b_ref, o_ref):
    o_ref[...] = _layer_norm(x_ref[...], g_ref[...], b_ref[...])


def _input_norm(x2, g, b):
    m = x2.shape[0]
    return pl.pallas_call(
        _ln_kernel,
        grid=(m // ROW_TILE,),
        in_specs=[pl.BlockSpec((ROW_TILE, D_MODEL), lambda i: (i, 0)),
                  pl.BlockSpec((1, D_MODEL), lambda i: (0, 0)),
                  pl.BlockSpec((1, D_MODEL), lambda i: (0, 0))],
        out_specs=pl.BlockSpec((ROW_TILE, D_MODEL), lambda i: (i, 0)),
        out_shape=jax.ShapeDtypeStruct((m, D_MODEL), F32),
        compiler_params=_cparams(("parallel",)),
        name="input_norm",
    )(x2, g.reshape(1, D_MODEL), b.reshape(1, D_MODEL))


def _inproj_kernel(h_ref, w_ref, b_ref, wvt_ref, bvt_ref,
                   uv_ref, bqkv_ref, cqk_ref, cvt_ref, iq_ref, ik_ref, iw_ref, mq_ref, gate_ref):
    xb = h_ref[...].astype(BF16)

    def seg(c):
        return jnp.dot(xb, w_ref[:, c[0]:c[1]], preferred_element_type=F32) + b_ref[:, c[0]:c[1]]

    uv_ref[...] = _gelu_tanh(seg(C_UV))
    bqkv_ref[...] = seg(C_BQKV).astype(BF16)
    cqk_ref[...] = seg(C_CQK).astype(BF16)
    iq_ref[...] = seg(C_IQ).astype(BF16)
    ik_ref[...] = seg(C_IK).astype(BF16)
    iw_ref[...] = seg(C_IW)
    mq_ref[...] = seg(C_MQ).astype(BF16)
    gate_ref[...] = _silu(seg(C_GATE))
    for t in range(ROW_TILE // TILE):
        vt = _nt_dot(wvt_ref[...], xb[t * TILE:(t + 1) * TILE, :]) + bvt_ref[...]
        cvt_ref[t] = vt.astype(BF16)


def _in_projection(h, w_all, b_all, wvt, bvt, bsz, seq):
    m = h.shape[0]
    steps_per_batch = seq // ROW_TILE
    tiles_per_step = ROW_TILE // TILE

    def rows(width, dtype):
        return (pl.BlockSpec((ROW_TILE, width), lambda i: (i, 0)),
                jax.ShapeDtypeStruct((m, width), dtype))

    outs = [rows(512, F32), rows(768, BF16), rows(512, BF16),
            (pl.BlockSpec((None, tiles_per_step, GROUP_W, TILE),
                          lambda i: (i // steps_per_batch, i % steps_per_batch, 0, 0)),
             jax.ShapeDtypeStruct((bsz, seq // TILE, GROUP_W, TILE), BF16)),
            rows(512, BF16), rows(256, BF16), rows(128, F32), rows(256, BF16), rows(1024, F32)]
    return pl.pallas_call(
        _inproj_kernel,
        grid=(m // ROW_TILE,),
        in_specs=[pl.BlockSpec((ROW_TILE, D_MODEL), lambda i: (i, 0)),
                  pl.BlockSpec((D_MODEL, N_PROJ), lambda i: (0, 0)),
                  pl.BlockSpec((1, N_PROJ), lambda i: (0, 0)),
                  pl.BlockSpec((GROUP_W, D_MODEL), lambda i: (0, 0)),
                  pl.BlockSpec((GROUP_W, 1), lambda i: (0, 0))],
        out_specs=[o[0] for o in outs],
        out_shape=[o[1] for o in outs],
        compiler_params=_cparams(("parallel",)),
        name="in_projection",
    )(h, w_all, b_all, wvt, bvt)


def _gmlp_kernel(uv_ref, gate_ref, lng_ref, lnb_ref, ws_ref, bs_ref, o_ref):
    lane_group = lax.broadcasted_iota(I32, (A_BLOCK, GROUP_W), 1) // HEAD_DIM
    for blk in range(ROW_TILE // A_BLOCK):
        r = slice(blk * A_BLOCK, (blk + 1) * A_BLOCK)
        u = uv_ref[r, 0:GROUP_W]
        v = _layer_norm(uv_ref[r, GROUP_W:2 * GROUP_W], lng_ref[...], lnb_ref[...])
        vb = v.astype(BF16)
        mixed = bs_ref[...]
        for g in range(N_HEADS):
            mg = jnp.dot(ws_ref[g], vb, preferred_element_type=F32)
            mixed = mixed + jnp.where(lane_group == g, mg, 0.0)
        o_ref[r, :] = ((u * mixed) * gate_ref[r, :]).astype(BF16)


def _gmlp(uv, gates, lng, lnb, ws_masked, bs_lanes):
    m = uv.shape[0]
    return pl.pallas_call(
        _gmlp_kernel,
        grid=(m // ROW_TILE,),
        in_specs=[pl.BlockSpec((ROW_TILE, 512), lambda i: (i, 0)),
                  pl.BlockSpec((ROW_TILE, GROUP_W), lambda i: (i, 0)),
                  pl.BlockSpec((1, GROUP_W), lambda i: (0, 0)),
                  pl.BlockSpec((1, GROUP_W), lambda i: (0, 0)),
                  pl.BlockSpec((N_HEADS, A_BLOCK, A_BLOCK), lambda i: (0, 0, 0)),
                  pl.BlockSpec((A_BLOCK, GROUP_W), lambda i: (0, 0))],
        out_specs=pl.BlockSpec((ROW_TILE, GROUP_W), lambda i: (i, 0)),
        out_shape=jax.ShapeDtypeStruct((m, GROUP_W), BF16),
        compiler_params=_cparams(("parallel",)),
        name="gmlp",
    )(uv, gates, lng, lnb, ws_masked, bs_lanes)


def _band_kernel(qkv_ref, gate_ref, bias_ref, o_ref):
    i = pl.program_id(1)
    q = qkv_ref[i, :, 0:GROUP_W]
    lane_head = lax.broadcasted_iota(I32, (TILE, GROUP_W), 1) // HEAD_DIM
    ks, vs, live = [], [], []
    for d in (2, 1, 0):
        kb = i - d
        kbc = jnp.maximum(kb, 0)
        ks.append(qkv_ref[kbc, :, GROUP_W:2 * GROUP_W])
        vs.append(qkv_ref[kbc, :, 2 * GROUP_W:3 * GROUP_W])
        live.append(kb >= 0)
    out = jnp.zeros((TILE, GROUP_W), F32)
    for h in range(N_HEADS):
        qh = jnp.where(lane_head == h, q, jnp.zeros_like(q))
        ss = []
        for t in range(3):
            s = _nt_dot(qh, ks[t]) + bias_ref[h, :, t * TILE:(t + 1) * TILE]
            ss.append(jnp.where(live[t], s, NEG_INF))
        mx = jnp.maximum(jnp.maximum(jnp.max(ss[0], axis=1, keepdims=True),
                                     jnp.max(ss[1], axis=1, keepdims=True)),
                         jnp.max(ss[2], axis=1, keepdims=True))
        den = jnp.zeros((TILE, 1), F32)
        pv = jnp.zeros((TILE, GROUP_W), F32)
        for t in range(3):
            p = jnp.exp(ss[t] - mx)
            den = den + jnp.sum(p, axis=1, keepdims=True)
            pv = pv + jnp.dot(p.astype(BF16), vs[t], preferred_element_type=F32)
        out = jnp.where(lane_head == h, pv * (1.0 / den), out)
    o_ref[...] = (out * gate_ref[...]).astype(BF16)


def _band_attention(bqkv4, gates, bias, bsz, seq):
    nt = seq // TILE
    return pl.pallas_call(
        _band_kernel,
        grid=(bsz, nt),
        in_specs=[pl.BlockSpec((None, nt, TILE, 768), lambda b, i: (b, 0, 0, 0)),
                  pl.BlockSpec((TILE, GROUP_W), lambda b, i: (b * nt + i, 1)),
                  pl.BlockSpec((N_HEADS, TILE, 3 * TILE), lambda b, i: (0, 0, 0))],
        out_specs=pl.BlockSpec((TILE, GROUP_W), lambda b, i: (b * nt + i, 0)),
        out_shape=jax.ShapeDtypeStruct((bsz * seq, GROUP_W), BF16),
        compiler_params=_cparams(("parallel", "arbitrary")),
        name="band_attention",
    )(bqkv4, gates, bias)


def _memkv_kernel(mem_ref, w_ref, km_ref, vm_ref):
    kv = jnp.dot(mem_ref[...].astype(BF16), w_ref[...], preferred_element_type=F32)
    km_ref[...] = kv[:, 0:GROUP_W].astype(BF16)
    vm_ref[...] = kv[:, GROUP_W:2 * GROUP_W].astype(BF16)


def _memory_kv(mem, w_kv):
    bsz, n_mem, _ = mem.shape
    blk = pl.BlockSpec((None, n_mem, GROUP_W), lambda b: (b, 0, 0))
    shp = jax.ShapeDtypeStruct((bsz, n_mem, GROUP_W), BF16)
    return pl.pallas_call(
        _memkv_kernel,
        grid=(bsz,),
        in_specs=[pl.BlockSpec((None, n_mem, D_MODEL), lambda b: (b, 0, 0)),
                  pl.BlockSpec((D_MODEL, 2 * GROUP_W), lambda b: (0, 0))],
        out_specs=[blk, blk],
        out_shape=[shp, shp],
        compiler_params=_cparams(("parallel",)),
        name="memory_kv",
    )(mem, w_kv)


def _memattn_kernel(q_ref, km_ref, vm_ref, gate_ref, o_ref):
    q = q_ref[...]
    lane_head = lax.broadcasted_iota(I32, (ROW_TILE, GROUP_W), 1) // HEAD_DIM
    out = jnp.zeros((ROW_TILE, GROUP_W), F32)
    for h in range(N_HEADS):
        qh = jnp.where(lane_head == h, q, jnp.zeros_like(q))
        s = _nt_dot(qh, km_ref[...])
        p = jnp.exp(s - jnp.max(s, axis=1, keepdims=True))
        den = jnp.sum(p, axis=1, keepdims=True)
        pv = jnp.dot(p.astype(BF16), vm_ref[...], preferred_element_type=F32)
        out = jnp.where(lane_head == h, pv * (1.0 / den), out)
    o_ref[...] = (out * gate_ref[...]).astype(BF16)


def _memory_attention(mq, km, vm, gates, bsz, seq):
    steps = seq // ROW_TILE
    n_mem = km.shape[1]
    return pl.pallas_call(
        _memattn_kernel,
        grid=(bsz, steps),
        in_specs=[pl.BlockSpec((ROW_TILE, GROUP_W), lambda b, i: (b * steps + i, 0)),
                  pl.BlockSpec((None, n_mem, GROUP_W), lambda b, i: (b, 0, 0)),
                  pl.BlockSpec((None, n_mem, GROUP_W), lambda b, i: (b, 0, 0)),
                  pl.BlockSpec((ROW_TILE, GROUP_W), lambda b, i: (b * steps + i, 3))],
        out_specs=pl.BlockSpec((ROW_TILE, GROUP_W), lambda b, i: (b * steps + i, 0)),
        out_shape=jax.ShapeDtypeStruct((bsz * seq, GROUP_W), BF16),
        compiler_params=_cparams(("parallel", "arbitrary")),
        name="memory_attention",
    )(mq, km, vm, gates)


def _dsa_kernel(cqk_ref, cvt_ref, ik_ref, iq_ref, iw_ref, gate_ref, t5_ref, tri_ref, o_ref,
                key_ref, am_ref, m_ref, l_ref, acc_ref):
    n = pl.program_id(1)
    q0 = n * TILE
    score_scale = (IDX_DIM ** -0.5) * (IDX_HEADS ** -0.5)

    key_pos = lax.broadcasted_iota(I32, (TILE, 1), 0)
    q_limit = ((q0 + lax.broadcasted_iota(I32, (1, TILE), 1)) // CHUNK + 1) * CHUNK

    iw_t = jnp.transpose(iw_ref[...]) * score_scale
    iq_pairs = [iq_ref[:, p * 128:(p + 1) * 128] for p in range(IDX_HEADS // 2)]

    def score_tile(j, carry):
        sc = jnp.zeros((TILE, TILE), F32)
        for p in range(IDX_HEADS // 2):
            for parity in range(2):
                ikp = ik_ref[j, :, parity * 128:(parity + 1) * 128]
                logit = _nt_dot(ikp, iq_pairs[p])
                h = 2 * p + parity
                sc = sc + jnp.maximum(logit, 0.0) * iw_t[h:h + 1, :]
        sc = jnp.where(sc == 0.0, 0.0, sc)
        sc = jnp.where(j * TILE + key_pos < q_limit, sc, NEG_INF)
        bits = pltpu.bitcast(sc, I32)
        key_ref[j] = bits ^ (lax.shift_right_arithmetic(bits, 31) & 0x7FFFFFFF)
        return carry

    lax.fori_loop(0, n + 1, score_tile, 0)

    def count(pred_fn):
        def body(j, c):
            ind = jnp.where(pred_fn(key_ref[j]), 1, 0).astype(I32)
            return c + jnp.sum(ind.reshape(TILE // 8, 8, TILE), axis=0)
        c = lax.fori_loop(0, n + 1, body, jnp.zeros((8, TILE), I32))
        return jnp.sum(c, axis=0, keepdims=True)

    def bit_step(t, prefix):
        cand_u = prefix | lax.shift_left(jnp.int32(1), 31 - t)
        cand = cand_u ^ INT_MIN
        total = count(lambda k: k >= cand)
        return jnp.where(total >= TOPK, cand_u, prefix)

    thr = lax.fori_loop(0, 32, bit_step, jnp.zeros((1, TILE), I32)) ^ INT_MIN
    n_above = count(lambda k: k > thr)
    quota = (TOPK - n_above).astype(F32)

    def mask_tile(j, taken):
        k = key_ref[j]
        eq = jnp.where(k == thr, 1.0, 0.0).astype(BF16)
        before = jnp.dot(tri_ref[...], eq, preferred_element_type=F32) + taken
        open_ = jnp.where(j * TILE + key_pos < q_limit, 0.0, NEG_INF)
        tied = jnp.where(k == thr, jnp.where(before < quota, open_, NEG_INF), NEG_INF)
        am_ref[j] = jnp.where(k > thr, open_, tied)
        return taken + jnp.sum(eq.astype(F32), axis=0, keepdims=True)

    lax.fori_loop(0, n + 1, mask_tile, jnp.zeros((1, TILE), F32))

    m_ref[...] = jnp.full(m_ref.shape, -1e30, F32)
    l_ref[...] = jnp.zeros(l_ref.shape, F32)
    acc_ref[...] = jnp.zeros(acc_ref.shape, F32)
    q = cqk_ref[n, :, 0:GROUP_W]
    lane_head = lax.broadcasted_iota(I32, (TILE, GROUP_W), 1) // HEAD_DIM
    q_heads = [jnp.where(lane_head == h, q, jnp.zeros_like(q)) for h in range(N_HEADS)]

    def attend(j, bias_rows):
        k = cqk_ref[j, :, GROUP_W:2 * GROUP_W]
        am = am_ref[j]
        for h in range(N_HEADS):
            s = _nt_dot(k, q_heads[h]) + am
            if bias_rows is not None:
                s = s + t5_ref[h, bias_rows[0]:bias_rows[1], :]
            m_old = m_ref[h]
            m_new = jnp.maximum(m_old, jnp.max(s, axis=0, keepdims=True))
            alpha = jnp.exp(m_old - m_new)
            p = jnp.exp(s - m_new)
            l_ref[h] = alpha * l_ref[h] + jnp.sum(p, axis=0, keepdims=True)
            vt = cvt_ref[j, h * HEAD_DIM:(h + 1) * HEAD_DIM, :]
            acc_ref[h] = alpha * acc_ref[h] + jnp.dot(vt, p.astype(BF16), preferred_element_type=F32)
            m_ref[h] = m_new

    def far_tile(j, carry):
        attend(j, None)
        return carry

    lax.fori_loop(0, n - 1, far_tile, 0)

    @pl.when(n >= 1)
    def _():
        attend(n - 1, (0, TILE))

    attend(n, (TILE, 2 * TILE))

    out_t = jnp.concatenate([acc_ref[h] * (1.0 / l_ref[h]) for h in range(N_HEADS)], axis=0)
    o_ref[...] = (jnp.transpose(out_t) * gate_ref[...]).astype(BF16)


def _dsa_attention(cqk4, cvt, ik4, iq, iw, gates, t5_rel, tri, bsz, seq):
    nt = seq // TILE
    return pl.pallas_call(
        _dsa_kernel,
        grid=(bsz, nt),
        in_specs=[pl.BlockSpec((None, nt, TILE, 512), lambda b, i: (b, 0, 0, 0)),
                  pl.BlockSpec((None, nt, GROUP_W, TILE), lambda b, i: (b, 0, 0, 0)),
                  pl.BlockSpec((None, nt, TILE, 256), lambda b, i: (b, 0, 0, 0)),
                  pl.BlockSpec((TILE, 512), lambda b, i: (b * nt + i, 0)),
                  pl.BlockSpec((TILE, 128), lambda b, i: (b * nt + i, 0)),
                  pl.BlockSpec((TILE, GROUP_W), lambda b, i: (b * nt + i, 2)),
                  pl.BlockSpec((N_HEADS, 2 * TILE, TILE), lambda b, i: (0, 0, 0)),
                  pl.BlockSpec((TILE, TILE), lambda b, i: (0, 0))],
        out_specs=pl.BlockSpec((TILE, GROUP_W), lambda b, i: (b * nt + i, 0)),
        out_shape=jax.ShapeDtypeStruct((bsz * seq, GROUP_W), BF16),
        scratch_shapes=[pltpu.VMEM((nt, TILE, TILE), I32),
                        pltpu.VMEM((nt, TILE, TILE), F32),
                        pltpu.VMEM((N_HEADS, 1, TILE), F32),
                        pltpu.VMEM((N_HEADS, 1, TILE), F32),
                        pltpu.VMEM((N_HEADS, HEAD_DIM, TILE), F32)],
        compiler_params=_cparams(("parallel", "arbitrary")),
        name="dsa_attention",
    )(cqk4, cvt, ik4, iq, iw, gates, t5_rel, tri)


def _outproj_kernel(h_ref, ya_ref, yb_ref, yc_ref, ym_ref, w_ref, b_ref, g_ref, beta_ref, o_ref):
    y = b_ref[...]
    for idx, part in enumerate((ya_ref, yb_ref, yc_ref, ym_ref)):
        y = y + jnp.dot(part[...], w_ref[idx * GROUP_W:(idx + 1) * GROUP_W, :],
                        preferred_element_type=F32)
    o_ref[...] = _layer_norm(DEEPNORM_ALPHA * h_ref[...] + y, g_ref[...], beta_ref[...])


def _out_projection(h, ya, yb, yc, ym, w_out, b_out, g, beta):
    m = h.shape[0]
    part = pl.BlockSpec((ROW_TILE, GROUP_W), lambda i: (i, 0))
    vec = pl.BlockSpec((1, D_MODEL), lambda i: (0, 0))
    return pl.pallas_call(
        _outproj_kernel,
        grid=(m // ROW_TILE,),
        in_specs=[pl.BlockSpec((ROW_TILE, D_MODEL), lambda i: (i, 0)), part, part, part, part,
                  pl.BlockSpec((D_MODEL, D_MODEL), lambda i: (0, 0)), vec, vec, vec],
        out_specs=pl.BlockSpec((ROW_TILE, D_MODEL), lambda i: (i, 0)),
        out_shape=jax.ShapeDtypeStruct((m, D_MODEL), F32),
        compiler_params=_cparams(("parallel",)),
        name="out_projection",
    )(h, ya, yb, yc, ym, w_out, b_out, g, beta)


def _prep_in_weights(w_in, b_in):
    sp = np.cumsum((0, 256, 256, 256, 256, 256, 256, 256, 256, 256, 256, 256, 512, 64, 8, 256, 256))
    names = ("a_u", "a_v", "a_g", "bq", "bk", "bv", "bg", "cq", "ck", "cv", "cg", "iq", "ik", "iw", "mq", "mg")
    col = {nm: slice(int(sp[i]), int(sp[i + 1])) for i, nm in enumerate(names)}
    qs = HEAD_DIM ** -0.5

    def build(t):
        z = lambda width: jnp.zeros((t.shape[0], width), t.dtype)
        ik = t[:, col["ik"]]
        parts = [t[:, col["a_u"]], t[:, col["a_v"]],
                 t[:, col["bq"]] * qs, t[:, col["bk"]], t[:, col["bv"]],
                 t[:, col["cq"]] * qs, t[:, col["ck"]],
                 t[:, col["iq"]],
                 ik, z(64), z(64), ik,
                 t[:, col["iw"]], z(120),
                 t[:, col["mq"]] * qs,
                 t[:, col["a_g"]], t[:, col["bg"]], t[:, col["cg"]], t[:, col["mg"]]]
        return jnp.concatenate(parts, axis=1), t[:, col["cv"]]

    w_all, w_cv = build(w_in)
    b_all, b_cv = build(b_in.reshape(1, -1))
    return (w_all.astype(BF16), b_all, jnp.transpose(w_cv).astype(BF16), jnp.transpose(b_cv))


def _band_bias_table(b_rel):
    i = np.arange(TILE)[:, None]
    j = np.arange(3 * TILE)[None, :]
    rel = i - j + 2 * TILE
    rel_idx = np.clip(rel, -REL_CLIP, REL_CLIP) + REL_CLIP
    chunk_diff = i // CHUNK - j // CHUNK + B_PREV_CHUNKS
    allowed = (chunk_diff >= 0) & (chunk_diff <= B_PREV_CHUNKS)
    return jnp.where(jnp.asarray(allowed)[None], b_rel[:, jnp.asarray(rel_idx)], NEG_INF).astype(F32)


def _t5_bucket(rel):
    nb = T5_BUCKETS // 2
    max_exact = nb // 2
    ret = jnp.where(rel > 0, nb, 0)
    n = jnp.abs(rel)
    nf = jnp.maximum(n, 1).astype(F32)
    large = max_exact + (jnp.log(nf / max_exact) / math.log(T5_MAX_DIST / max_exact)
                         * (nb - max_exact)).astype(I32)
    large = jnp.minimum(large, nb - 1)
    return ret + jnp.where(n < max_exact, n, large)


def _t5_rel_table(t5_table):
    kk = jnp.arange(2 * TILE, dtype=I32)[:, None]
    qi = jnp.arange(TILE, dtype=I32)[None, :]
    rel = kk - TILE - qi
    near = t5_table[_t5_bucket(rel)]
    far = t5_table[_t5_bucket(jnp.full((1, 1), -(2 * TILE + 1), I32))]
    return jnp.transpose(near - far, (2, 0, 1)).astype(F32)


def kernel(x, mem, ln_in_g, ln_in_b, w_in, b_in, a_ln_g, a_ln_b, a_ws, a_bs, b_rel, t5_table,
           w_mem_kv, w_out, b_out, ln_g, ln_b):
    bsz, seq, _ = x.shape
    assert seq % ROW_TILE == 0 and x.shape[2] == D_MODEL
    m = bsz * seq
    nt = seq // TILE

    cpos = np.arange(A_BLOCK) // CHUNK
    a_mask = jnp.asarray(cpos[None, :] <= cpos[:, None])
    t5_rel = _t5_rel_table(t5_table)
    tri = jnp.asarray(np.tril(np.ones((TILE, TILE), np.float32), -1)).astype(BF16)

    h = _input_norm(x.reshape(m, D_MODEL), ln_in_g, ln_in_b)
    for l in range(DEPTH):
        w_all, b_all, wvt, bvt = _prep_in_weights(w_in[l], b_in[l])
        uv, bqkv, cqk, cvt, iq, ik, iw, mq, gates = _in_projection(h, w_all, b_all, wvt, bvt, bsz, seq)

        ws_masked = jnp.where(a_mask[None], a_ws[l], 0.0).astype(BF16)
        bs_lanes = jnp.repeat(jnp.transpose(a_bs[l]), HEAD_DIM, axis=1)
        ya = _gmlp(uv, gates, a_ln_g[l].reshape(1, -1), a_ln_b[l].reshape(1, -1), ws_masked, bs_lanes)

        yb = _band_attention(bqkv.reshape(bsz, nt, TILE, 768), gates, _band_bias_table(b_rel[l]), bsz, seq)

        yc = _dsa_attention(cqk.reshape(bsz, nt, TILE, 512), cvt, ik.reshape(bsz, nt, TILE, 256),
                            iq, iw, gates, t5_rel, tri, bsz, seq)

        km, vm = _memory_kv(mem, w_mem_kv[l].astype(BF16))
        ym = _memory_attention(mq, km, vm, gates, bsz, seq)

        h = _out_projection(h, ya, yb, yc, ym, w_out[l].astype(BF16), b_out[l].reshape(1, -1),
                            ln_g[l].reshape(1, -1), ln_b[l].reshape(1, -1))
    return h.reshape(bsz, seq, D_MODEL)
```

```python
import functools
import math

import numpy as np
import jax
import jax.numpy as jnp
from jax import lax
from jax.experimental import pallas as pl
from jax.experimental.pallas import tpu as pltpu

F32 = jnp.float32
BF16 = jnp.bfloat16
I32 = jnp.int32

D_MODEL = 1024
CHUNK = 64
HEAD_DIM = 64
GROUP_W = 256
N_HEADS = GROUP_W // HEAD_DIM
A_BLOCK = 128
B_PREV_CHUNKS = 8
REL_CLIP = 128
IDX_HEADS = 8
IDX_DIM = 64
TOPK = 256
T5_BUCKETS = 32
T5_MAX_DIST = 128
N_MEM = 256
DEPTH = 2
DEEPNORM_ALPHA = (2 * DEPTH) ** 0.25
LN_EPS = 1e-5

TILE = 256
ROW_TILE = 512
VMEM_LIMIT = 56 * 1024 * 1024
NEG_INF = float("-inf")
INT_MIN = -2 ** 31

C_UV = (0, 512)
C_BQKV = (512, 1280)
C_CQK = (1280, 1792)
C_IQ = (1792, 2304)
C_IK = (2304, 2560)
C_IW = (2560, 2688)
C_MQ = (2688, 2944)
C_GATE = (2944, 3968)
N_PROJ = 3968


def _cparams(sem):
    return pltpu.CompilerParams(dimension_semantics=sem, vmem_limit_bytes=VMEM_LIMIT)


def _nt_dot(a, b):
    return lax.dot_general(a, b, (((1,), (1,)), ((), ())), preferred_element_type=F32)


def _gelu_tanh(x):
    return 0.5 * x * (1.0 + jnp.tanh(math.sqrt(2.0 / math.pi) * (x + 0.044715 * (x * x * x))))


def _silu(x):
    return x * (1.0 / (1.0 + jnp.exp(-x)))


def _layer_norm(x, g, b):
    mu = jnp.mean(x, axis=-1, keepdims=True)
    xc = x - mu
    var = jnp.mean(xc * xc, axis=-1, keepdims=True)
    return xc * lax.rsqrt(var + LN_EPS) * g + b


def _ln_kernel(x_ref, g_ref, b_ref, o_ref):
    o_ref[...] = _layer_norm(x_ref[...], g_ref[...], b_ref[...])


def _input_norm(x2, g, b):
    m = x2.shape[0]
    return pl.pallas_call(
        _ln_kernel,
        grid=(m // ROW_TILE,),
        in_specs=[pl.BlockSpec((ROW_TILE, D_MODEL), lambda i: (i, 0)),
                  pl.BlockSpec((1, D_MODEL), lambda i: (0, 0)),
                  pl.BlockSpec((1, D_MODEL), lambda i: (0, 0))],
        out_specs=pl.BlockSpec((ROW_TILE, D_MODEL), lambda i: (i, 0)),
        out_shape=jax.ShapeDtypeStruct((m, D_MODEL), F32),
        compiler_params=_cparams(("parallel",)),
        name="input_norm",
    )(x2, g.reshape(1, D_MODEL), b.reshape(1, D_MODEL))


def _inproj_kernel(h_ref, w_ref, b_ref, wvt_ref, bvt_ref,
                   uv_ref, bqkv_ref, cqk_ref, cvt_ref, iq_ref, ik_ref, iw_ref, mq_ref, gate_ref):
    xb = h_ref[...].astype(BF16)

    def seg(c):
        return jnp.dot(xb, w_ref[:, c[0]:c[1]], preferred_element_type=F32) + b_ref[:, c[0]:c[1]]

    uv_ref[...] = _gelu_tanh(seg(C_UV))
    bqkv_ref[...] = seg(C_BQKV).astype(BF16)
    cqk_ref[...] = seg(C_CQK).astype(BF16)
    iq_ref[...] = seg(C_IQ).astype(BF16)
    ik_ref[...] = seg(C_IK).astype(BF16)
    iw_ref[...] = seg(C_IW)
    mq_ref[...] = seg(C_MQ).astype(BF16)
    gate_ref[...] = _silu(seg(C_GATE))
    for t in range(ROW_TILE // TILE):
        vt = _nt_dot(wvt_ref[...], xb[t * TILE:(t + 1) * TILE, :]) + bvt_ref[...]
        cvt_ref[t] = vt.astype(BF16)


def _in_projection(h, w_all, b_all, wvt, bvt, bsz, seq):
    m = h.shape[0]
    steps_per_batch = seq // ROW_TILE
    tiles_per_step = ROW_TILE // TILE

    def rows(width, dtype):
        return (pl.BlockSpec((ROW_TILE, width), lambda i: (i, 0)),
                jax.ShapeDtypeStruct((m, width), dtype))

    outs = [rows(512, F32), rows(768, BF16), rows(512, BF16),
            (pl.BlockSpec((None, tiles_per_step, GROUP_W, TILE),
                          lambda i: (i // steps_per_batch, i % steps_per_batch, 0, 0)),
             jax.ShapeDtypeStruct((bsz, seq // TILE, GROUP_W, TILE), BF16)),
            rows(512, BF16), rows(256, BF16), rows(128, F32), rows(256, BF16), rows(1024, F32)]
    return pl.pallas_call(
        _inproj_kernel,
        grid=(m // ROW_TILE,),
        in_specs=[pl.BlockSpec((ROW_TILE, D_MODEL), lambda i: (i, 0)),
                  pl.BlockSpec((D_MODEL, N_PROJ), lambda i: (0, 0)),
                  pl.BlockSpec((1, N_PROJ), lambda i: (0, 0)),
                  pl.BlockSpec((GROUP_W, D_MODEL), lambda i: (0, 0)),
                  pl.BlockSpec((GROUP_W, 1), lambda i: (0, 0))],
        out_specs=[o[0] for o in outs],
        out_shape=[o[1] for o in outs],
        compiler_params=_cparams(("parallel",)),
        name="in_projection",
    )(h, w_all, b_all, wvt, bvt)


def _gmlp_kernel(uv_ref, gate_ref, lng_ref, lnb_ref, ws_ref, bs_ref, o_ref):
    lane_group = lax.broadcasted_iota(I32, (A_BLOCK, GROUP_W), 1) // HEAD_DIM
    for blk in range(ROW_TILE // A_BLOCK):
        r = slice(blk * A_BLOCK, (blk + 1) * A_BLOCK)
        u = uv_ref[r, 0:GROUP_W]
        v = _layer_norm(uv_ref[r, GROUP_W:2 * GROUP_W], lng_ref[...], lnb_ref[...])
        vb = v.astype(BF16)
        mixed = bs_ref[...]
        for g in range(N_HEADS):
            mg = jnp.dot(ws_ref[g], vb, preferred_element_type=F32)
            mixed = mixed + jnp.where(lane_group == g, mg, 0.0)
        o_ref[r, :] = ((u * mixed) * gate_ref[r, :]).astype(BF16)


def _gmlp(uv, gates, lng, lnb, ws_masked, bs_lanes):
    m = uv.shape[0]
    return pl.pallas_call(
        _gmlp_kernel,
        grid=(m // ROW_TILE,),
        in_specs=[pl.BlockSpec((ROW_TILE, 512), lambda i: (i, 0)),
                  pl.BlockSpec((ROW_TILE, GROUP_W), lambda i: (i, 0)),
                  pl.BlockSpec((1, GROUP_W), lambda i: (0, 0)),
                  pl.BlockSpec((1, GROUP_W), lambda i: (0, 0)),
                  pl.BlockSpec((N_HEADS, A_BLOCK, A_BLOCK), lambda i: (0, 0, 0)),
                  pl.BlockSpec((A_BLOCK, GROUP_W), lambda i: (0, 0))],
        out_specs=pl.BlockSpec((ROW_TILE, GROUP_W), lambda i: (i, 0)),
        out_shape=jax.ShapeDtypeStruct((m, GROUP_W), BF16),
        compiler_params=_cparams(("parallel",)),
        name="gmlp",
    )(uv, gates, lng, lnb, ws_masked, bs_lanes)


def _band_kernel(qkv_ref, gate_ref, bias_ref, o_ref):
    i = pl.program_id(1)
    q = qkv_ref[i, :, 0:GROUP_W]
    lane_head = lax.broadcasted_iota(I32, (TILE, GROUP_W), 1) // HEAD_DIM
    ks, vs, live = [], [], []
    for d in (2, 1, 0):
        kb = i - d
        kbc = jnp.maximum(kb, 0)
        ks.append(qkv_ref[kbc, :, GROUP_W:2 * GROUP_W])
        vs.append(qkv_ref[kbc, :, 2 * GROUP_W:3 * GROUP_W])
        live.append(kb >= 0)
    out = jnp.zeros((TILE, GROUP_W), F32)
    for h in range(N_HEADS):
        qh = jnp.where(lane_head == h, q, jnp.zeros_like(q))
        ss = []
        for t in range(3):
            s = _nt_dot(qh, ks[t]) + bias_ref[h, :, t * TILE:(t + 1) * TILE]
            ss.append(jnp.where(live[t], s, NEG_INF))
        mx = jnp.maximum(jnp.maximum(jnp.max(ss[0], axis=1, keepdims=True),
                                     jnp.max(ss[1], axis=1, keepdims=True)),
                         jnp.max(ss[2], axis=1, keepdims=True))
        den = jnp.zeros((TILE, 1), F32)
        pv = jnp.zeros((TILE, GROUP_W), F32)
        for t in range(3):
            p = jnp.exp(ss[t] - mx)
            den = den + jnp.sum(p, axis=1, keepdims=True)
            pv = pv + jnp.dot(p.astype(BF16), vs[t], preferred_element_type=F32)
        out = jnp.where(lane_head == h, pv * (1.0 / den), out)
    o_ref[...] = (out * gate_ref[...]).astype(BF16)


def _band_attention(bqkv4, gates, bias, bsz, seq):
    nt = seq // TILE
    return pl.pallas_call(
        _band_kernel,
        grid=(bsz, nt),
        in_specs=[pl.BlockSpec((None, nt, TILE, 768), lambda b, i: (b, 0, 0, 0)),
                  pl.BlockSpec((TILE, GROUP_W), lambda b, i: (b * nt + i, 1)),
                  pl.BlockSpec((N_HEADS, TILE, 3 * TILE), lambda b, i: (0, 0, 0))],
        out_specs=pl.BlockSpec((TILE, GROUP_W), lambda b, i: (b * nt + i, 0)),
        out_shape=jax.ShapeDtypeStruct((bsz * seq, GROUP_W), BF16),
        compiler_params=_cparams(("parallel", "arbitrary")),
        name="band_attention",
    )(bqkv4, gates, bias)


def _memkv_kernel(mem_ref, w_ref, km_ref, vm_ref):
    kv = jnp.dot(mem_ref[...].astype(BF16), w_ref[...], preferred_element_type=F32)
    km_ref[...] = kv[:, 0:GROUP_W].astype(BF16)
    vm_ref[...] = kv[:, GROUP_W:2 * GROUP_W].astype(BF16)


def _memory_kv(mem, w_kv):
    bsz, n_mem, _ = mem.shape
    blk = pl.BlockSpec((None, n_mem, GROUP_W), lambda b: (b, 0, 0))
    shp = jax.ShapeDtypeStruct((bsz, n_mem, GROUP_W), BF16)
    return pl.pallas_call(
        _memkv_kernel,
        grid=(bsz,),
        in_specs=[pl.BlockSpec((None, n_mem, D_MODEL), lambda b: (b, 0, 0)),
                  pl.BlockSpec((D_MODEL, 2 * GROUP_W), lambda b: (0, 0))],
        out_specs=[blk, blk],
        out_shape=[shp, shp],
        compiler_params=_cparams(("parallel",)),
        name="memory_kv",
    )(mem, w_kv)


def _memattn_kernel(q_ref, km_ref, vm_ref, gate_ref, o_ref):
    q = q_ref[...]
    lane_head = lax.broadcasted_iota(I32, (ROW_TILE, GROUP_W), 1) // HEAD_DIM
    out = jnp.zeros((ROW_TILE, GROUP_W), F32)
    for h in range(N_HEADS):
        qh = jnp.where(lane_head == h, q, jnp.zeros_like(q))
        s = _nt_dot(qh, km_ref[...])
        p = jnp.exp(s - jnp.max(s, axis=1, keepdims=True))
        den = jnp.sum(p, axis=1, keepdims=True)
        pv = jnp.dot(p.astype(BF16), vm_ref[...], preferred_element_type=F32)
        out = jnp.where(lane_head == h, pv * (1.0 / den), out)
    o_ref[...] = (out * gate_ref[...]).astype(BF16)


def _memory_attention(mq, km, vm, gates, bsz, seq):
    steps = seq // ROW_TILE
    n_mem = km.shape[1]
    return pl.pallas_call(
        _memattn_kernel,
        grid=(bsz, steps),
        in_specs=[pl.BlockSpec((ROW_TILE, GROUP_W), lambda b, i: (b * steps + i, 0)),
                  pl.BlockSpec((None, n_mem, GROUP_W), lambda b, i: (b, 0, 0)),
                  pl.BlockSpec((None, n_mem, GROUP_W), lambda b, i: (b, 0, 0)),
                  pl.BlockSpec((ROW_TILE, GROUP_W), lambda b, i: (b * steps + i, 3))],
        out_specs=pl.BlockSpec((ROW_TILE, GROUP_W), lambda b, i: (b * steps + i, 0)),
        out_shape=jax.ShapeDtypeStruct((bsz * seq, GROUP_W), BF16),
        compiler_params=_cparams(("parallel", "arbitrary")),
        name="memory_attention",
    )(mq, km, vm, gates)


def _dsa_kernel(cqk_ref, cvt_ref, ik_ref, iq_ref, iw_ref, gate_ref, t5_ref, tri_ref, o_ref,
                key_ref, am_ref, m_ref, l_ref, acc_ref, s_scr, p_scr):
    n = pl.program_id(1)
    q0 = n * TILE
    score_scale = (IDX_DIM ** -0.5) * (IDX_HEADS ** -0.5)

    key_pos = lax.broadcasted_iota(I32, (TILE, 1), 0)
    q_limit = ((q0 + lax.broadcasted_iota(I32, (1, TILE), 1)) // CHUNK + 1) * CHUNK

    iw_t = jnp.transpose(iw_ref[...]) * score_scale
    iq_pairs = [iq_ref[:, p * 128:(p + 1) * 128] for p in range(IDX_HEADS // 2)]

    def score_tile(j, carry):
        sc = jnp.zeros((TILE, TILE), F32)
        for p in range(IDX_HEADS // 2):
            for parity in range(2):
                ikp = ik_ref[j, :, parity * 128:(parity + 1) * 128]
                logit = _nt_dot(ikp, iq_pairs[p])
                h = 2 * p + parity
                sc = sc + jnp.maximum(logit, 0.0) * iw_t[h:h + 1, :]
        sc = jnp.where(sc == 0.0, 0.0, sc)
        sc = jnp.where(j * TILE + key_pos < q_limit, sc, NEG_INF)
        bits = pltpu.bitcast(sc, I32)
        key_ref[j] = bits ^ (lax.shift_right_arithmetic(bits, 31) & 0x7FFFFFFF)
        return carry

    lax.fori_loop(0, n + 1, score_tile, 0)

    def count(pred_fn):
        def body(j, c):
            ind = jnp.where(pred_fn(key_ref[j]), 1, 0).astype(I32)
            return c + jnp.sum(ind.reshape(TILE // 8, 8, TILE), axis=0)
        c = lax.fori_loop(0, n + 1, body, jnp.zeros((8, TILE), I32))
        return jnp.sum(c, axis=0, keepdims=True)

    def bit_step(t, prefix):
        cand_u = prefix | lax.shift_left(jnp.int32(1), 31 - t)
        cand = cand_u ^ INT_MIN
        total = count(lambda k: k >= cand)
        return jnp.where(total >= TOPK, cand_u, prefix)

    thr = lax.fori_loop(0, 32, bit_step, jnp.zeros((1, TILE), I32)) ^ INT_MIN
    n_above = count(lambda k: k > thr)
    quota = (TOPK - n_above).astype(F32)

    def mask_tile(j, taken):
        k = key_ref[j]
        eq = jnp.where(k == thr, 1.0, 0.0).astype(BF16)
        before = jnp.dot(tri_ref[...], eq, preferred_element_type=F32) + taken
        open_ = jnp.where(j * TILE + key_pos < q_limit, 0.0, NEG_INF)
        tied = jnp.where(k == thr, jnp.where(before < quota, open_, NEG_INF), NEG_INF)
        am_ref[j] = jnp.where(k > thr, open_, tied)
        return taken + jnp.sum(eq.astype(F32), axis=0, keepdims=True)

    lax.fori_loop(0, n + 1, mask_tile, jnp.zeros((1, TILE), F32))

    m_ref[...] = jnp.full(m_ref.shape, -1e30, F32)
    l_ref[...] = jnp.zeros(l_ref.shape, F32)
    acc_ref[...] = jnp.zeros(acc_ref.shape, F32)
    q = cqk_ref[n, :, 0:GROUP_W]
    lane_head = lax.broadcasted_iota(I32, (TILE, GROUP_W), 1) // HEAD_DIM
    q_heads = [jnp.where(lane_head == h, q, jnp.zeros_like(q)) for h in range(N_HEADS)]

    def attend(j, bias_rows):
        k = cqk_ref[j, :, GROUP_W:2 * GROUP_W]
        am = am_ref[j]
        m_new, alpha = [], []
        for h in range(N_HEADS):
            s = _nt_dot(k, q_heads[h]) + am
            if bias_rows is not None:
                s = s + t5_ref[h, bias_rows[0]:bias_rows[1], :]
            s_scr[h] = s
            m_old = m_ref[h]
            m_new.append(jnp.maximum(m_old, jnp.max(s, axis=0, keepdims=True)))
            alpha.append(jnp.exp(m_old - m_new[h]))
            m_ref[h] = m_new[h]
        for h in range(N_HEADS):
            p = jnp.exp(s_scr[h] - m_new[h])
            l_ref[h] = alpha[h] * l_ref[h] + jnp.sum(p, axis=0, keepdims=True)
            p_scr[h] = p.astype(BF16)
        for h in range(N_HEADS):
            vt = cvt_ref[j, h * HEAD_DIM:(h + 1) * HEAD_DIM, :]
            acc_ref[h] = alpha[h] * acc_ref[h] + jnp.dot(vt, p_scr[h], preferred_element_type=F32)

    def far_tile(j, carry):
        attend(j, None)
        return carry

    lax.fori_loop(0, n - 1, far_tile, 0)

    @pl.when(n >= 1)
    def _():
        attend(n - 1, (0, TILE))

    attend(n, (TILE, 2 * TILE))

    out_t = jnp.concatenate([acc_ref[h] * (1.0 / l_ref[h]) for h in range(N_HEADS)], axis=0)
    o_ref[...] = (jnp.transpose(out_t) * gate_ref[...]).astype(BF16)


def _dsa_attention(cqk4, cvt, ik4, iq, iw, gates, t5_rel, tri, bsz, seq):
    nt = seq // TILE
    return pl.pallas_call(
        _dsa_kernel,
        grid=(bsz, nt),
        in_specs=[pl.BlockSpec((None, nt, TILE, 512), lambda b, i: (b, 0, 0, 0)),
                  pl.BlockSpec((None, nt, GROUP_W, TILE), lambda b, i: (b, 0, 0, 0)),
                  pl.BlockSpec((None, nt, TILE, 256), lambda b, i: (b, 0, 0, 0)),
                  pl.BlockSpec((TILE, 512), lambda b, i: (b * nt + i, 0)),
                  pl.BlockSpec((TILE, 128), lambda b, i: (b * nt + i, 0)),
                  pl.BlockSpec((TILE, GROUP_W), lambda b, i: (b * nt + i, 2)),
                  pl.BlockSpec((N_HEADS, 2 * TILE, TILE), lambda b, i: (0, 0, 0)),
                  pl.BlockSpec((TILE, TILE), lambda b, i: (0, 0))],
        out_specs=pl.BlockSpec((TILE, GROUP_W), lambda b, i: (b * nt + i, 0)),
        out_shape=jax.ShapeDtypeStruct((bsz * seq, GROUP_W), BF16),
        scratch_shapes=[pltpu.VMEM((nt, TILE, TILE), I32),
                        pltpu.VMEM((nt, TILE, TILE), F32),
                        pltpu.VMEM((N_HEADS, 1, TILE), F32),
                        pltpu.VMEM((N_HEADS, 1, TILE), F32),
                        pltpu.VMEM((N_HEADS, HEAD_DIM, TILE), F32),
                        pltpu.VMEM((N_HEADS, TILE, TILE), F32),
                        pltpu.VMEM((N_HEADS, TILE, TILE), BF16)],
        compiler_params=_cparams(("parallel", "arbitrary")),
        name="dsa_attention",
    )(cqk4, cvt, ik4, iq, iw, gates, t5_rel, tri)


def _outproj_kernel(h_ref, ya_ref, yb_ref, yc_ref, ym_ref, w_ref, b_ref, g_ref, beta_ref, o_ref):
    y = b_ref[...]
    for idx, part in enumerate((ya_ref, yb_ref, yc_ref, ym_ref)):
        y = y + jnp.dot(part[...], w_ref[idx * GROUP_W:(idx + 1) * GROUP_W, :],
                        preferred_element_type=F32)
    o_ref[...] = _layer_norm(DEEPNORM_ALPHA * h_ref[...] + y, g_ref[...], beta_ref[...])


def _out_projection(h, ya, yb, yc, ym, w_out, b_out, g, beta):
    m = h.shape[0]
    part = pl.BlockSpec((ROW_TILE, GROUP_W), lambda i: (i, 0))
    vec = pl.BlockSpec((1, D_MODEL), lambda i: (0, 0))
    return pl.pallas_call(
        _outproj_kernel,
        grid=(m // ROW_TILE,),
        in_specs=[pl.BlockSpec((ROW_TILE, D_MODEL), lambda i: (i, 0)), part, part, part, part,
                  pl.BlockSpec((D_MODEL, D_MODEL), lambda i: (0, 0)), vec, vec, vec],
        out_specs=pl.BlockSpec((ROW_TILE, D_MODEL), lambda i: (i, 0)),
        out_shape=jax.ShapeDtypeStruct((m, D_MODEL), F32),
        compiler_params=_cparams(("parallel",)),
        name="out_projection",
    )(h, ya, yb, yc, ym, w_out, b_out, g, beta)


def _prep_in_weights(w_in, b_in):
    sp = np.cumsum((0, 256, 256, 256, 256, 256, 256, 256, 256, 256, 256, 256, 512, 64, 8, 256, 256))
    names = ("a_u", "a_v", "a_g", "bq", "bk", "bv", "bg", "cq", "ck", "cv", "cg", "iq", "ik", "iw", "mq", "mg")
    col = {nm: slice(int(sp[i]), int(sp[i + 1])) for i, nm in enumerate(names)}
    qs = HEAD_DIM ** -0.5

    def build(t):
        z = lambda width: jnp.zeros((t.shape[0], width), t.dtype)
        ik = t[:, col["ik"]]
        parts = [t[:, col["a_u"]], t[:, col["a_v"]],
                 t[:, col["bq"]] * qs, t[:, col["bk"]], t[:, col["bv"]],
                 t[:, col["cq"]] * qs, t[:, col["ck"]],
                 t[:, col["iq"]],
                 ik, z(64), z(64), ik,
                 t[:, col["iw"]], z(120),
                 t[:, col["mq"]] * qs,
                 t[:, col["a_g"]], t[:, col["bg"]], t[:, col["cg"]], t[:, col["mg"]]]
        return jnp.concatenate(parts, axis=1), t[:, col["cv"]]

    w_all, w_cv = build(w_in)
    b_all, b_cv = build(b_in.reshape(1, -1))
    return (w_all.astype(BF16), b_all, jnp.transpose(w_cv).astype(BF16), jnp.transpose(b_cv))


def _toeplitz(g, n_rows, n_cols):
    span = n_rows + n_cols - 1
    lead = g.shape[:-1]
    period = jnp.concatenate([g, jnp.zeros(lead + (1,), g.dtype)], axis=-1)
    flat = jnp.tile(period, (1,) * len(lead) + (n_rows,))[..., :n_rows * span]
    return flat.reshape(lead + (n_rows, span))[..., n_rows - 1:n_rows - 1 + n_cols]


def _band_bias_table(b_rel):
    m = np.arange(4 * TILE - 1)
    rel_idx = np.clip(3 * TILE - 1 - m, -REL_CLIP, REL_CLIP) + REL_CLIP
    bias = _toeplitz(b_rel[:, jnp.asarray(rel_idx)], TILE, 3 * TILE)
    i = np.arange(TILE)[:, None]
    j = np.arange(3 * TILE)[None, :]
    chunk_diff = i // CHUNK - j // CHUNK + B_PREV_CHUNKS
    allowed = (chunk_diff >= 0) & (chunk_diff <= B_PREV_CHUNKS)
    return jnp.where(jnp.asarray(allowed)[None], bias, NEG_INF).astype(F32)


def _t5_bucket(rel):
    nb = T5_BUCKETS // 2
    max_exact = nb // 2
    ret = jnp.where(rel > 0, nb, 0)
    n = jnp.abs(rel)
    nf = jnp.maximum(n, 1).astype(F32)
    large = max_exact + (jnp.log(nf / max_exact) / math.log(T5_MAX_DIST / max_exact)
                         * (nb - max_exact)).astype(I32)
    large = jnp.minimum(large, nb - 1)
    return ret + jnp.where(n < max_exact, n, large)


def _t5_rel_table(t5_table):
    m = jnp.arange(3 * TILE - 1, dtype=I32)
    near = t5_table[_t5_bucket(TILE - 1 - m)]
    far = t5_table[_t5_bucket(jnp.full((1,), -(2 * TILE + 1), I32))]
    return _toeplitz(jnp.transpose(near - far), 2 * TILE, TILE).astype(F32)


def kernel(x, mem, ln_in_g, ln_in_b, w_in, b_in, a_ln_g, a_ln_b, a_ws, a_bs, b_rel, t5_table,
           w_mem_kv, w_out, b_out, ln_g, ln_b):
    bsz, seq, _ = x.shape
    assert seq % ROW_TILE == 0 and x.shape[2] == D_MODEL
    m = bsz * seq
    nt = seq // TILE

    cpos = np.arange(A_BLOCK) // CHUNK
    a_mask = jnp.asarray(cpos[None, :] <= cpos[:, None])
    t5_rel = _t5_rel_table(t5_table)
    tri = jnp.asarray(np.tril(np.ones((TILE, TILE), np.float32), -1)).astype(BF16)

    h = _input_norm(x.reshape(m, D_MODEL), ln_in_g, ln_in_b)
    for l in range(DEPTH):
        w_all, b_all, wvt, bvt = _prep_in_weights(w_in[l], b_in[l])
        uv, bqkv, cqk, cvt, iq, ik, iw, mq, gates = _in_projection(h, w_all, b_all, wvt, bvt, bsz, seq)

        ws_masked = jnp.where(a_mask[None], a_ws[l], 0.0).astype(BF16)
        bs_lanes = jnp.repeat(jnp.transpose(a_bs[l]), HEAD_DIM, axis=1)
        ya = _gmlp(uv, gates, a_ln_g[l].reshape(1, -1), a_ln_b[l].reshape(1, -1), ws_masked, bs_lanes)

        yb = _band_attention(bqkv.reshape(bsz, nt, TILE, 768), gates, _band_bias_table(b_rel[l]), bsz, seq)

        yc = _dsa_attention(cqk.reshape(bsz, nt, TILE, 512), cvt, ik.reshape(bsz, nt, TILE, 256),
                            iq, iw, gates, t5_rel, tri, bsz, seq)

        km, vm = _memory_kv(mem, w_mem_kv[l].astype(BF16))
        ym = _memory_attention(mq, km, vm, gates, bsz, seq)

        h = _out_projection(h, ya, yb, yc, ym, w_out[l].astype(BF16), b_out[l].reshape(1, -1),
                            ln_g[l].reshape(1, -1), ln_b[l].reshape(1, -1))
    return h.reshape(bsz, seq, D_MODEL)
```

```python
import functools
import math

import numpy as np
import jax
import jax.numpy as jnp
from jax import lax
from jax.experimental import pallas as pl
from jax.experimental.pallas import tpu as pltpu

F32 = jnp.float32
BF16 = jnp.bfloat16
I32 = jnp.int32
I16 = jnp.int16

D_MODEL = 1024
CHUNK = 64
HEAD_DIM = 64
GROUP_W = 256
N_HEADS = GROUP_W // HEAD_DIM
A_BLOCK = 128
B_PREV_CHUNKS = 8
REL_CLIP = 128
IDX_HEADS = 8
IDX_DIM = 64
TOPK = 256
T5_BUCKETS = 32
T5_MAX_DIST = 128
N_MEM = 256
DEPTH = 2
DEEPNORM_ALPHA = (2 * DEPTH) ** 0.25
LN_EPS = 1e-5

TILE = 256
ROW_TILE = 512
VMEM_LIMIT = 56 * 1024 * 1024
NEG_INF = float("-inf")
HALF_BIAS = 2 ** 15
COUNT_ROWS = 64
ROW_CHUNK = 64
LOG2E = 1.4426950408889634

C_UV = (0, 512)
C_BQKV = (512, 1280)
C_CQK = (1280, 1792)
C_IQ = (1792, 2304)
C_IK = (2304, 2560)
C_IW = (2560, 2688)
C_MQ = (2688, 2944)
C_GATE = (2944, 3968)
N_PROJ = 3968


def _cparams(sem):
    return pltpu.CompilerParams(dimension_semantics=sem, vmem_limit_bytes=VMEM_LIMIT)


def _nt_dot(a, b):
    return lax.dot_general(a, b, (((1,), (1,)), ((), ())), preferred_element_type=F32)


def _gelu_tanh(x):
    return 0.5 * x * (1.0 + jnp.tanh(math.sqrt(2.0 / math.pi) * (x + 0.044715 * (x * x * x))))


def _silu(x):
    return x * (1.0 / (1.0 + jnp.exp(-x)))


def _layer_norm(x, g, b):
    mu = jnp.mean(x, axis=-1, keepdims=True)
    xc = x - mu
    var = jnp.mean(xc * xc, axis=-1, keepdims=True)
    return xc * lax.rsqrt(var + LN_EPS) * g + b


def _ln_kernel(x_ref, g_ref, b_ref, o_ref):
    o_ref[...] = _layer_norm(x_ref[...], g_ref[...], b_ref[...])


def _input_norm(x2, g, b):
    m = x2.shape[0]
    return pl.pallas_call(
        _ln_kernel,
        grid=(m // ROW_TILE,),
        in_specs=[pl.BlockSpec((ROW_TILE, D_MODEL), lambda i: (i, 0)),
                  pl.BlockSpec((1, D_MODEL), lambda i: (0, 0)),
                  pl.BlockSpec((1, D_MODEL), lambda i: (0, 0))],
        out_specs=pl.BlockSpec((ROW_TILE, D_MODEL), lambda i: (i, 0)),
        out_shape=jax.ShapeDtypeStruct((m, D_MODEL), F32),
        compiler_params=_cparams(("parallel",)),
        name="input_norm",
    )(x2, g.reshape(1, D_MODEL), b.reshape(1, D_MODEL))


def _inproj_kernel(h_ref, w_ref, b_ref, wvt_ref, bvt_ref,
                   uv_ref, bqkv_ref, cqk_ref, cvt_ref, iq_ref, ik_ref, iw_ref, mq_ref, gate_ref):
    xb = h_ref[...].astype(BF16)

    def seg(c):
        return jnp.dot(xb, w_ref[:, c[0]:c[1]], preferred_element_type=F32) + b_ref[:, c[0]:c[1]]

    uv_ref[...] = _gelu_tanh(seg(C_UV))
    bqkv_ref[...] = seg(C_BQKV).astype(BF16)
    cqk_ref[...] = seg(C_CQK).astype(BF16)
    iq_ref[...] = seg(C_IQ).astype(BF16)
    ik_ref[...] = seg(C_IK).astype(BF16)
    iw_ref[...] = seg(C_IW)
    mq_ref[...] = seg(C_MQ).astype(BF16)
    gate_ref[...] = _silu(seg(C_GATE))
    for t in range(ROW_TILE // TILE):
        vt = _nt_dot(wvt_ref[...], xb[t * TILE:(t + 1) * TILE, :]) + bvt_ref[...]
        cvt_ref[t] = vt.astype(BF16)


def _in_projection(h, w_all, b_all, wvt, bvt, bsz, seq):
    m = h.shape[0]
    steps_per_batch = seq // ROW_TILE
    tiles_per_step = ROW_TILE // TILE

    def rows(width, dtype):
        return (pl.BlockSpec((ROW_TILE, width), lambda i: (i, 0)),
                jax.ShapeDtypeStruct((m, width), dtype))

    outs = [rows(512, F32), rows(768, BF16), rows(512, BF16),
            (pl.BlockSpec((None, tiles_per_step, GROUP_W, TILE),
                          lambda i: (i // steps_per_batch, i % steps_per_batch, 0, 0)),
             jax.ShapeDtypeStruct((bsz, seq // TILE, GROUP_W, TILE), BF16)),
            rows(512, BF16), rows(256, BF16), rows(128, F32), rows(256, BF16), rows(1024, F32)]
    return pl.pallas_call(
        _inproj_kernel,
        grid=(m // ROW_TILE,),
        in_specs=[pl.BlockSpec((ROW_TILE, D_MODEL), lambda i: (i, 0)),
                  pl.BlockSpec((D_MODEL, N_PROJ), lambda i: (0, 0)),
                  pl.BlockSpec((1, N_PROJ), lambda i: (0, 0)),
                  pl.BlockSpec((GROUP_W, D_MODEL), lambda i: (0, 0)),
                  pl.BlockSpec((GROUP_W, 1), lambda i: (0, 0))],
        out_specs=[o[0] for o in outs],
        out_shape=[o[1] for o in outs],
        compiler_params=_cparams(("parallel",)),
        name="in_projection",
    )(h, w_all, b_all, wvt, bvt)


def _gmlp_kernel(uv_ref, gate_ref, lng_ref, lnb_ref, ws_ref, bs_ref, o_ref):
    lane_group = lax.broadcasted_iota(I32, (A_BLOCK, GROUP_W), 1) // HEAD_DIM
    for blk in range(ROW_TILE // A_BLOCK):
        r = slice(blk * A_BLOCK, (blk + 1) * A_BLOCK)
        u = uv_ref[r, 0:GROUP_W]
        v = _layer_norm(uv_ref[r, GROUP_W:2 * GROUP_W], lng_ref[...], lnb_ref[...])
        vb = v.astype(BF16)
        mixed = bs_ref[...]
        for g in range(N_HEADS):
            mg = jnp.dot(ws_ref[g], vb, preferred_element_type=F32)
            mixed = mixed + jnp.where(lane_group == g, mg, 0.0)
        o_ref[r, :] = ((u * mixed) * gate_ref[r, :]).astype(BF16)


def _gmlp(uv, gates, lng, lnb, ws_masked, bs_lanes):
    m = uv.shape[0]
    return pl.pallas_call(
        _gmlp_kernel,
        grid=(m // ROW_TILE,),
        in_specs=[pl.BlockSpec((ROW_TILE, 512), lambda i: (i, 0)),
                  pl.BlockSpec((ROW_TILE, GROUP_W), lambda i: (i, 0)),
                  pl.BlockSpec((1, GROUP_W), lambda i: (0, 0)),
                  pl.BlockSpec((1, GROUP_W), lambda i: (0, 0)),
                  pl.BlockSpec((N_HEADS, A_BLOCK, A_BLOCK), lambda i: (0, 0, 0)),
                  pl.BlockSpec((A_BLOCK, GROUP_W), lambda i: (0, 0))],
        out_specs=pl.BlockSpec((ROW_TILE, GROUP_W), lambda i: (i, 0)),
        out_shape=jax.ShapeDtypeStruct((m, GROUP_W), BF16),
        compiler_params=_cparams(("parallel",)),
        name="gmlp",
    )(uv, gates, lng, lnb, ws_masked, bs_lanes)


def _band_kernel(qkv_ref, gate_ref, bias_ref, o_ref):
    i = pl.program_id(1)
    q = qkv_ref[i, :, 0:GROUP_W]
    lane_head = lax.broadcasted_iota(I32, (TILE, GROUP_W), 1) // HEAD_DIM
    ks, vs, live = [], [], []
    for d in (2, 1, 0):
        kb = i - d
        kbc = jnp.maximum(kb, 0)
        ks.append(qkv_ref[kbc, :, GROUP_W:2 * GROUP_W])
        vs.append(qkv_ref[kbc, :, 2 * GROUP_W:3 * GROUP_W])
        live.append(kb >= 0)
    out = jnp.zeros((TILE, GROUP_W), F32)
    for h in range(N_HEADS):
        qh = jnp.where(lane_head == h, q, jnp.zeros_like(q))
        ss = []
        for t in range(3):
            s = _nt_dot(qh, ks[t]) + bias_ref[h, :, t * TILE:(t + 1) * TILE]
            ss.append(jnp.where(live[t], s, NEG_INF))
        mx = jnp.maximum(jnp.maximum(jnp.max(ss[0], axis=1, keepdims=True),
                                     jnp.max(ss[1], axis=1, keepdims=True)),
                         jnp.max(ss[2], axis=1, keepdims=True))
        den = jnp.zeros((TILE, 1), F32)
        pv = jnp.zeros((TILE, GROUP_W), F32)
        for t in range(3):
            p = jnp.exp(ss[t] - mx)
            den = den + jnp.sum(p, axis=1, keepdims=True)
            pv = pv + jnp.dot(p.astype(BF16), vs[t], preferred_element_type=F32)
        out = jnp.where(lane_head == h, pv * (1.0 / den), out)
    o_ref[...] = (out * gate_ref[...]).astype(BF16)


def _band_attention(bqkv4, gates, bias, bsz, seq):
    nt = seq // TILE
    return pl.pallas_call(
        _band_kernel,
        grid=(bsz, nt),
        in_specs=[pl.BlockSpec((None, nt, TILE, 768), lambda b, i: (b, 0, 0, 0)),
                  pl.BlockSpec((TILE, GROUP_W), lambda b, i: (b * nt + i, 1)),
                  pl.BlockSpec((N_HEADS, TILE, 3 * TILE), lambda b, i: (0, 0, 0))],
        out_specs=pl.BlockSpec((TILE, GROUP_W), lambda b, i: (b * nt + i, 0)),
        out_shape=jax.ShapeDtypeStruct((bsz * seq, GROUP_W), BF16),
        compiler_params=_cparams(("parallel", "arbitrary")),
        name="band_attention",
    )(bqkv4, gates, bias)


def _memkv_kernel(mem_ref, w_ref, km_ref, vm_ref):
    kv = jnp.dot(mem_ref[...].astype(BF16), w_ref[...], preferred_element_type=F32)
    km_ref[...] = kv[:, 0:GROUP_W].astype(BF16)
    vm_ref[...] = kv[:, GROUP_W:2 * GROUP_W].astype(BF16)


def _memory_kv(mem, w_kv):
    bsz, n_mem, _ = mem.shape
    blk = pl.BlockSpec((None, n_mem, GROUP_W), lambda b: (b, 0, 0))
    shp = jax.ShapeDtypeStruct((bsz, n_mem, GROUP_W), BF16)
    return pl.pallas_call(
        _memkv_kernel,
        grid=(bsz,),
        in_specs=[pl.BlockSpec((None, n_mem, D_MODEL), lambda b: (b, 0, 0)),
                  pl.BlockSpec((D_MODEL, 2 * GROUP_W), lambda b: (0, 0))],
        out_specs=[blk, blk],
        out_shape=[shp, shp],
        compiler_params=_cparams(("parallel",)),
        name="memory_kv",
    )(mem, w_kv)


def _memattn_kernel(q_ref, km_ref, vm_ref, gate_ref, o_ref):
    q = q_ref[...]
    lane_head = lax.broadcasted_iota(I32, (ROW_TILE, GROUP_W), 1) // HEAD_DIM
    out = jnp.zeros((ROW_TILE, GROUP_W), F32)
    for h in range(N_HEADS):
        qh = jnp.where(lane_head == h, q, jnp.zeros_like(q))
        s = _nt_dot(qh, km_ref[...])
        p = jnp.exp(s - jnp.max(s, axis=1, keepdims=True))
        den = jnp.sum(p, axis=1, keepdims=True)
        pv = jnp.dot(p.astype(BF16), vm_ref[...], preferred_element_type=F32)
        out = jnp.where(lane_head == h, pv * (1.0 / den), out)
    o_ref[...] = (out * gate_ref[...]).astype(BF16)


def _memory_attention(mq, km, vm, gates, bsz, seq):
    steps = seq // ROW_TILE
    n_mem = km.shape[1]
    return pl.pallas_call(
        _memattn_kernel,
        grid=(bsz, steps),
        in_specs=[pl.BlockSpec((ROW_TILE, GROUP_W), lambda b, i: (b * steps + i, 0)),
                  pl.BlockSpec((None, n_mem, GROUP_W), lambda b, i: (b, 0, 0)),
                  pl.BlockSpec((None, n_mem, GROUP_W), lambda b, i: (b, 0, 0)),
                  pl.BlockSpec((ROW_TILE, GROUP_W), lambda b, i: (b * steps + i, 3))],
        out_specs=pl.BlockSpec((ROW_TILE, GROUP_W), lambda b, i: (b * steps + i, 0)),
        out_shape=jax.ShapeDtypeStruct((bsz * seq, GROUP_W), BF16),
        compiler_params=_cparams(("parallel", "arbitrary")),
        name="memory_attention",
    )(mq, km, vm, gates)


def _dsa_kernel(cqk_ref, cvt_ref, ik_ref, iq_ref, iw_ref, gate_ref, t5_ref, tri_ref, o_ref,
                key_ref, hi_ref, lo_ref, am_ref, iwt_ref, qh_ref, lg_ref, sa_ref, sb_ref, pa_ref, pb_ref,
                m_ref, l_ref, alpha_ref, acc_ref):
    n = pl.program_id(1)
    nt = key_ref.shape[0]
    q0 = n * TILE
    n_pairs = n // 2 + 1
    score_scale = (IDX_DIM ** -0.5) * (IDX_HEADS ** -0.5)
    chunks = [slice(c * ROW_CHUNK, (c + 1) * ROW_CHUNK) for c in range(TILE // ROW_CHUNK)]

    q_limit = ((q0 + lax.broadcasted_iota(I32, (1, TILE), 1)) // CHUNK + 1) * CHUNK
    chunk_pos = lax.broadcasted_iota(I32, (ROW_CHUNK, 1), 0)

    def admissible(rows):
        return q0 + rows.start + chunk_pos < q_limit

    iwt_ref[...] = jnp.transpose(iw_ref[...])[0:IDX_HEADS, :] * score_scale

    def score_tile(j, diagonal):
        for p in range(IDX_HEADS // 2):
            for parity in range(2):
                lg_ref[2 * p + parity] = _nt_dot(ik_ref[j, :, parity * 128:(parity + 1) * 128],
                                                 iq_ref[:, p * 128:(p + 1) * 128])
        half = (j % 2) * TILE
        for rows in chunks:
            sc = None
            for h in range(IDX_HEADS):
                term = jnp.maximum(lg_ref[h, rows, :], 0.0) * iwt_ref[h:h + 1, :]
                sc = term if sc is None else sc + term
            if diagonal:
                sc = jnp.where(admissible(rows), sc, NEG_INF)
            bits = pltpu.bitcast(sc, I32)
            sign = lax.shift_right_arithmetic(bits, 31)
            key = (bits ^ (sign & 0x7FFFFFFF)) - sign
            key_ref[j, rows, :] = key
            dst = pl.ds(pl.multiple_of(half + rows.start, ROW_CHUNK), ROW_CHUNK)
            hi_ref[j // 2, dst, :] = lax.shift_right_arithmetic(key, 16).astype(I16)
            lo_ref[j // 2, dst, :] = (key ^ HALF_BIAS).astype(I16)

    def far_score_tile(j, carry):
        score_tile(j, False)
        return carry

    lax.fori_loop(0, n, far_score_tile, 0)
    score_tile(n, True)

    @pl.when(n % 2 == 0)
    def _():
        hi_ref[n // 2, TILE:2 * TILE, :] = jnp.full((TILE, TILE), -HALF_BIAS, I16)
        lo_ref[n // 2, TILE:2 * TILE, :] = jnp.full((TILE, TILE), -HALF_BIAS, I16)

    def count16(ref, pred_fn):
        def body(p, c):
            ind = jnp.where(pred_fn(ref[p]), jnp.int16(1), jnp.int16(0))
            for r in range(2 * TILE // COUNT_ROWS):
                c = c + ind[r * COUNT_ROWS:(r + 1) * COUNT_ROWS, :]
            return c
        c = lax.fori_loop(0, n_pairs, body, jnp.zeros((COUNT_ROWS, TILE), I16))
        return jnp.sum(c.astype(I32), axis=0, keepdims=True)

    def radix16(ref, n_fixed):
        def bit_step(t, prefix):
            cand_u = prefix | lax.shift_left(jnp.int32(1), 15 - t)
            cand = (cand_u - HALF_BIAS).astype(I16)
            total = n_fixed + count16(ref, lambda k: k >= cand)
            return jnp.where(total >= TOPK, cand_u, prefix)
        return lax.fori_loop(0, 16, bit_step, jnp.zeros((1, TILE), I32))

    hi_u = radix16(hi_ref, 0)
    thr_hi = (hi_u - HALF_BIAS).astype(I16)
    n_hi_above = count16(hi_ref, lambda k: k > thr_hi)

    def keep_matching(p, carry):
        lo_ref[p] = jnp.where(hi_ref[p] == thr_hi, lo_ref[p], jnp.int16(-HALF_BIAS))
        return carry

    lax.fori_loop(0, n_pairs, keep_matching, 0)
    lo_u = radix16(lo_ref, n_hi_above)
    thr_lo = (lo_u - HALF_BIAS).astype(I16)
    n_above = n_hi_above + count16(lo_ref, lambda k: k > thr_lo)
    n_equal = count16(lo_ref, lambda k: k == thr_lo)
    thr = (hi_u - HALF_BIAS) * (2 * HALF_BIAS) + lo_u
    quota = TOPK - n_above
    ties_cut = jnp.max(n_equal - quota) > 0

    am_ref[nt + 2 * N_HEADS] = jnp.full((TILE, TILE), NEG_INF, F32)

    def write_masks(with_ties):
        quota_f = quota.astype(F32)

        def selected(j, rows, taken, diagonal):
            k = key_ref[j, rows, :]
            open_ = jnp.where(admissible(rows), 0.0, NEG_INF) if diagonal else 0.0
            if not with_ties:
                return jnp.where(k >= thr, open_, NEG_INF), taken
            eq = jnp.where(k == thr, 1.0, 0.0).astype(BF16)
            before = jnp.dot(tri_ref[rows.start:rows.stop, rows.start:rows.stop], eq,
                             preferred_element_type=F32) + taken
            tied = jnp.where(k == thr, jnp.where(before < quota_f, open_, NEG_INF), NEG_INF)
            return jnp.where(k > thr, open_, tied), taken + jnp.sum(eq.astype(F32), axis=0, keepdims=True)

        def far_mask(j, taken):
            for rows in chunks:
                am, taken = selected(j, rows, taken, False)
                am_ref[j, rows, :] = am
            return taken

        taken = lax.fori_loop(0, n - 1, far_mask, jnp.zeros((1, TILE), F32))

        def near_mask(j, d, taken):
            for rows in chunks:
                am, taken = selected(j, rows, taken, d == 1)
                for h in range(N_HEADS):
                    am_ref[nt + N_HEADS * d + h, rows, :] = am + t5_ref[h, d * TILE + rows.start:d * TILE + rows.stop, :]
            return taken

        prev = near_mask(jnp.maximum(n - 1, 0), 0, taken)
        near_mask(n, 1, jnp.where(n >= 1, prev, taken))

    @pl.when(ties_cut)
    def _():
        write_masks(True)

    @pl.when(jnp.logical_not(ties_cut))
    def _():
        write_masks(False)

    m_ref[...] = jnp.full(m_ref.shape, -1e30, F32)
    l_ref[...] = jnp.zeros(l_ref.shape, F32)
    acc_ref[...] = jnp.zeros(acc_ref.shape, F32)
    q = cqk_ref[n, :, 0:GROUP_W]
    lane_head = lax.broadcasted_iota(I32, (TILE, GROUP_W), 1) // HEAD_DIM
    for h in range(N_HEADS):
        qh_ref[h] = jnp.where(lane_head == h, q, jnp.zeros_like(q))

    def logits(j, s_buf):
        k = cqk_ref[jnp.minimum(j, n), :, GROUP_W:2 * GROUP_W]
        for h in range(N_HEADS):
            s_buf[h] = _nt_dot(k, qh_ref[h])

    def softmax_step(j, s_buf, p_buf):
        near = j - (n - 1)
        for h in range(N_HEADS):
            slab = jnp.where(j > n, nt + 2 * N_HEADS, jnp.where(near >= 0, nt + N_HEADS * near + h, j))
            m_old = m_ref[h]
            mx = None
            for rows in chunks:
                s = s_buf[h, rows, :] + am_ref[slab, rows, :]
                s_buf[h, rows, :] = s
                mx = s if mx is None else jnp.maximum(mx, s)
            m_new = jnp.maximum(m_old, jnp.max(mx, axis=0, keepdims=True))
            alpha = jnp.exp2(m_old - m_new)
            tot = None
            for rows in chunks:
                p = jnp.exp2(s_buf[h, rows, :] - m_new)
                p_buf[h, rows, :] = p.astype(BF16)
                tot = p if tot is None else tot + p
            l_ref[h] = alpha * l_ref[h] + jnp.sum(tot, axis=0, keepdims=True)
            m_ref[h] = m_new
            alpha_ref[h] = alpha

    def accumulate(j, p_buf):
        jc = jnp.minimum(j, n)
        for h in range(N_HEADS):
            vt = cvt_ref[jc, h * HEAD_DIM:(h + 1) * HEAD_DIM, :]
            acc_ref[h] = alpha_ref[h] * acc_ref[h] + jnp.dot(vt, p_buf[h], preferred_element_type=F32)

    logits(0, sa_ref)

    def tile_pair(i, carry):
        j = 2 * i
        logits(j + 1, sb_ref)
        softmax_step(j, sa_ref, pa_ref)
        accumulate(j, pa_ref)
        logits(j + 2, sa_ref)
        softmax_step(j + 1, sb_ref, pb_ref)
        accumulate(j + 1, pb_ref)
        return carry

    lax.fori_loop(0, n_pairs, tile_pair, 0)

    out_t = jnp.concatenate([acc_ref[h] * (1.0 / l_ref[h]) for h in range(N_HEADS)], axis=0)
    o_ref[...] = (jnp.transpose(out_t) * gate_ref[...]).astype(BF16)


def _dsa_attention(cqk4, cvt, ik4, iq, iw, gates, t5_rel, tri, bsz, seq):
    nt = seq // TILE
    assert nt % 2 == 0
    return pl.pallas_call(
        _dsa_kernel,
        grid=(bsz, nt),
        in_specs=[pl.BlockSpec((None, nt, TILE, 512), lambda b, i: (b, 0, 0, 0)),
                  pl.BlockSpec((None, nt, GROUP_W, TILE), lambda b, i: (b, 0, 0, 0)),
                  pl.BlockSpec((None, nt, TILE, 256), lambda b, i: (b, 0, 0, 0)),
                  pl.BlockSpec((TILE, 512), lambda b, i: (b * nt + i, 0)),
                  pl.BlockSpec((TILE, 128), lambda b, i: (b * nt + i, 0)),
                  pl.BlockSpec((TILE, GROUP_W), lambda b, i: (b * nt + i, 2)),
                  pl.BlockSpec((N_HEADS, 2 * TILE, TILE), lambda b, i: (0, 0, 0)),
                  pl.BlockSpec((TILE, TILE), lambda b, i: (0, 0))],
        out_specs=pl.BlockSpec((TILE, GROUP_W), lambda b, i: (b * nt + i, 0)),
        out_shape=jax.ShapeDtypeStruct((bsz * seq, GROUP_W), BF16),
        scratch_shapes=[pltpu.VMEM((nt, TILE, TILE), I32),
                        pltpu.VMEM((nt // 2, 2 * TILE, TILE), I16),
                        pltpu.VMEM((nt // 2, 2 * TILE, TILE), I16),
                        pltpu.VMEM((nt + 2 * N_HEADS + 1, TILE, TILE), F32),
                        pltpu.VMEM((IDX_HEADS, TILE), F32),
                        pltpu.VMEM((N_HEADS, TILE, GROUP_W), BF16),
                        pltpu.VMEM((IDX_HEADS, TILE, TILE), F32),
                        pltpu.VMEM((N_HEADS, TILE, TILE), F32),
                        pltpu.VMEM((N_HEADS, TILE, TILE), F32),
                        pltpu.VMEM((N_HEADS, TILE, TILE), BF16),
                        pltpu.VMEM((N_HEADS, TILE, TILE), BF16),
                        pltpu.VMEM((N_HEADS, 1, TILE), F32),
                        pltpu.VMEM((N_HEADS, 1, TILE), F32),
                        pltpu.VMEM((N_HEADS, 1, TILE), F32),
                        pltpu.VMEM((N_HEADS, HEAD_DIM, TILE), F32)],
        compiler_params=_cparams(("parallel", "arbitrary")),
        name="dsa_attention",
    )(cqk4, cvt, ik4, iq, iw, gates, t5_rel, tri)


def _outproj_kernel(h_ref, ya_ref, yb_ref, yc_ref, ym_ref, w_ref, b_ref, g_ref, beta_ref, o_ref):
    y = b_ref[...]
    for idx, part in enumerate((ya_ref, yb_ref, yc_ref, ym_ref)):
        y = y + jnp.dot(part[...], w_ref[idx * GROUP_W:(idx + 1) * GROUP_W, :],
                        preferred_element_type=F32)
    o_ref[...] = _layer_norm(DEEPNORM_ALPHA * h_ref[...] + y, g_ref[...], beta_ref[...])


def _out_projection(h, ya, yb, yc, ym, w_out, b_out, g, beta):
    m = h.shape[0]
    part = pl.BlockSpec((ROW_TILE, GROUP_W), lambda i: (i, 0))
    vec = pl.BlockSpec((1, D_MODEL), lambda i: (0, 0))
    return pl.pallas_call(
        _outproj_kernel,
        grid=(m // ROW_TILE,),
        in_specs=[pl.BlockSpec((ROW_TILE, D_MODEL), lambda i: (i, 0)), part, part, part, part,
                  pl.BlockSpec((D_MODEL, D_MODEL), lambda i: (0, 0)), vec, vec, vec],
        out_specs=pl.BlockSpec((ROW_TILE, D_MODEL), lambda i: (i, 0)),
        out_shape=jax.ShapeDtypeStruct((m, D_MODEL), F32),
        compiler_params=_cparams(("parallel",)),
        name="out_projection",
    )(h, ya, yb, yc, ym, w_out, b_out, g, beta)


def _prep_in_weights(w_in, b_in):
    sp = np.cumsum((0, 256, 256, 256, 256, 256, 256, 256, 256, 256, 256, 256, 512, 64, 8, 256, 256))
    names = ("a_u", "a_v", "a_g", "bq", "bk", "bv", "bg", "cq", "ck", "cv", "cg", "iq", "ik", "iw", "mq", "mg")
    col = {nm: slice(int(sp[i]), int(sp[i + 1])) for i, nm in enumerate(names)}
    qs = HEAD_DIM ** -0.5

    def build(t):
        z = lambda width: jnp.zeros((t.shape[0], width), t.dtype)
        ik = t[:, col["ik"]]
        parts = [t[:, col["a_u"]], t[:, col["a_v"]],
                 t[:, col["bq"]] * qs, t[:, col["bk"]], t[:, col["bv"]],
                 t[:, col["cq"]] * (qs * LOG2E), t[:, col["ck"]],
                 t[:, col["iq"]],
                 ik, z(64), z(64), ik,
                 t[:, col["iw"]], z(120),
                 t[:, col["mq"]] * qs,
                 t[:, col["a_g"]], t[:, col["bg"]], t[:, col["cg"]], t[:, col["mg"]]]
        return jnp.concatenate(parts, axis=1), t[:, col["cv"]]

    w_all, w_cv = build(w_in)
    b_all, b_cv = build(b_in.reshape(1, -1))
    return (w_all.astype(BF16), b_all, jnp.transpose(w_cv).astype(BF16), jnp.transpose(b_cv))


def _toeplitz(g, n_rows, n_cols):
    span = n_rows + n_cols - 1
    lead = g.shape[:-1]
    period = jnp.concatenate([g, jnp.zeros(lead + (1,), g.dtype)], axis=-1)
    flat = jnp.tile(period, (1,) * len(lead) + (n_rows,))[..., :n_rows * span]
    return flat.reshape(lead + (n_rows, span))[..., n_rows - 1:n_rows - 1 + n_cols]


def _band_bias_table(b_rel):
    m = np.arange(4 * TILE - 1)
    rel_idx = np.clip(3 * TILE - 1 - m, -REL_CLIP, REL_CLIP) + REL_CLIP
    bias = _toeplitz(b_rel[:, jnp.asarray(rel_idx)], TILE, 3 * TILE)
    i = np.arange(TILE)[:, None]
    j = np.arange(3 * TILE)[None, :]
    chunk_diff = i // CHUNK - j // CHUNK + B_PREV_CHUNKS
    allowed = (chunk_diff >= 0) & (chunk_diff <= B_PREV_CHUNKS)
    return jnp.where(jnp.asarray(allowed)[None], bias, NEG_INF).astype(F32)


def _t5_bucket(rel):
    nb = T5_BUCKETS // 2
    max_exact = nb // 2
    ret = jnp.where(rel > 0, nb, 0)
    n = jnp.abs(rel)
    nf = jnp.maximum(n, 1).astype(F32)
    large = max_exact + (jnp.log(nf / max_exact) / math.log(T5_MAX_DIST / max_exact)
                         * (nb - max_exact)).astype(I32)
    large = jnp.minimum(large, nb - 1)
    return ret + jnp.where(n < max_exact, n, large)


def _t5_rel_table(t5_table):
    m = jnp.arange(3 * TILE - 1, dtype=I32)
    near = t5_table[_t5_bucket(TILE - 1 - m)]
    far = t5_table[_t5_bucket(jnp.full((1,), -(2 * TILE + 1), I32))]
    return _toeplitz(jnp.transpose(near - far) * LOG2E, 2 * TILE, TILE).astype(F32)


def kernel(x, mem, ln_in_g, ln_in_b, w_in, b_in, a_ln_g, a_ln_b, a_ws, a_bs, b_rel, t5_table,
           w_mem_kv, w_out, b_out, ln_g, ln_b):
    bsz, seq, _ = x.shape
    assert seq % ROW_TILE == 0 and x.shape[2] == D_MODEL
    m = bsz * seq
    nt = seq // TILE

    cpos = np.arange(A_BLOCK) // CHUNK
    a_mask = jnp.asarray(cpos[None, :] <= cpos[:, None])
    t5_rel = _t5_rel_table(t5_table)
    tri = jnp.asarray(np.tril(np.ones((TILE, TILE), np.float32), -1)).astype(BF16)

    h = _input_norm(x.reshape(m, D_MODEL), ln_in_g, ln_in_b)
    for l in range(DEPTH):
        w_all, b_all, wvt, bvt = _prep_in_weights(w_in[l], b_in[l])
        uv, bqkv, cqk, cvt, iq, ik, iw, mq, gates = _in_projection(h, w_all, b_all, wvt, bvt, bsz, seq)

        ws_masked = jnp.where(a_mask[None], a_ws[l], 0.0).astype(BF16)
        bs_lanes = jnp.repeat(jnp.transpose(a_bs[l]), HEAD_DIM, axis=1)
        ya = _gmlp(uv, gates, a_ln_g[l].reshape(1, -1), a_ln_b[l].reshape(1, -1), ws_masked, bs_lanes)

        yb = _band_attention(bqkv.reshape(bsz, nt, TILE, 768), gates, _band_bias_table(b_rel[l]), bsz, seq)

        yc = _dsa_attention(cqk.reshape(bsz, nt, TILE, 512), cvt, ik.reshape(bsz, nt, TILE, 256),
                            iq, iw, gates, t5_rel, tri, bsz, seq)

        km, vm = _memory_kv(mem, w_mem_kv[l].astype(BF16))
        ym = _memory_attention(mq, km, vm, gates, bsz, seq)

        h = _out_projection(h, ya, yb, yc, ym, w_out[l].astype(BF16), b_out[l].reshape(1, -1),
                            ln_g[l].reshape(1, -1), ln_b[l].reshape(1, -1))
    return h.reshape(bsz, seq, D_MODEL)
```

```python
import functools
import math

import numpy as np
import jax
import jax.numpy as jnp
from jax import lax
from jax.experimental import pallas as pl
from jax.experimental.pallas import tpu as pltpu

F32 = jnp.float32
BF16 = jnp.bfloat16
I32 = jnp.int32

D_MODEL = 1024
CHUNK = 64
HEAD_DIM = 64
GROUP_W = 256
N_HEADS = GROUP_W // HEAD_DIM
A_BLOCK = 128
B_PREV_CHUNKS = 8
REL_CLIP = 128
IDX_HEADS = 8
IDX_DIM = 64
TOPK = 256
T5_BUCKETS = 32
T5_MAX_DIST = 128
N_MEM = 256
DEPTH = 2
DEEPNORM_ALPHA = (2 * DEPTH) ** 0.25
LN_EPS = 1e-5

TILE = 256
ROW_TILE = 512
VMEM_LIMIT = 56 * 1024 * 1024
NEG_INF = float("-inf")
SIGN_BIT = -2 ** 31
SUBLANES = 8
GROUP_TILES = 4
ROW_CHUNK = 64
LOG2E = 1.4426950408889634

C_UV = (0, 512)
C_BQKV = (512, 1280)
C_CQK = (1280, 1792)
C_IQ = (1792, 2304)
C_IK = (2304, 2560)
C_IW = (2560, 2688)
C_MQ = (2688, 2944)
C_GATE = (2944, 3968)
N_PROJ = 3968


def _cparams(sem):
    return pltpu.CompilerParams(dimension_semantics=sem, vmem_limit_bytes=VMEM_LIMIT)


def _nt_dot(a, b):
    return lax.dot_general(a, b, (((1,), (1,)), ((), ())), preferred_element_type=F32)


def _gelu_tanh(x):
    return 0.5 * x * (1.0 + jnp.tanh(math.sqrt(2.0 / math.pi) * (x + 0.044715 * (x * x * x))))


def _silu(x):
    return x * (1.0 / (1.0 + jnp.exp(-x)))


def _layer_norm(x, g, b):
    mu = jnp.mean(x, axis=-1, keepdims=True)
    xc = x - mu
    var = jnp.mean(xc * xc, axis=-1, keepdims=True)
    return xc * lax.rsqrt(var + LN_EPS) * g + b


def _ln_kernel(x_ref, g_ref, b_ref, o_ref):
    o_ref[...] = _layer_norm(x_ref[...], g_ref[...], b_ref[...])


def _input_norm(x2, g, b):
    m = x2.shape[0]
    return pl.pallas_call(
        _ln_kernel,
        grid=(m // ROW_TILE,),
        in_specs=[pl.BlockSpec((ROW_TILE, D_MODEL), lambda i: (i, 0)),
                  pl.BlockSpec((1, D_MODEL), lambda i: (0, 0)),
                  pl.BlockSpec((1, D_MODEL), lambda i: (0, 0))],
        out_specs=pl.BlockSpec((ROW_TILE, D_MODEL), lambda i: (i, 0)),
        out_shape=jax.ShapeDtypeStruct((m, D_MODEL), F32),
        compiler_params=_cparams(("parallel",)),
        name="input_norm",
    )(x2, g.reshape(1, D_MODEL), b.reshape(1, D_MODEL))


def _inproj_kernel(h_ref, w_ref, b_ref, wvt_ref, bvt_ref,
                   uv_ref, bqkv_ref, cqk_ref, cvt_ref, iq_ref, ik_ref, iw_ref, mq_ref, gate_ref):
    xb = h_ref[...].astype(BF16)

    def seg(c):
        return jnp.dot(xb, w_ref[:, c[0]:c[1]], preferred_element_type=F32) + b_ref[:, c[0]:c[1]]

    uv_ref[...] = _gelu_tanh(seg(C_UV))
    bqkv_ref[...] = seg(C_BQKV).astype(BF16)
    cqk_ref[...] = seg(C_CQK).astype(BF16)
    iq_ref[...] = seg(C_IQ).astype(BF16)
    ik_ref[...] = seg(C_IK).astype(BF16)
    iw_ref[...] = seg(C_IW)
    mq_ref[...] = seg(C_MQ).astype(BF16)
    gate_ref[...] = _silu(seg(C_GATE))
    for t in range(ROW_TILE // TILE):
        vt = _nt_dot(wvt_ref[...], xb[t * TILE:(t + 1) * TILE, :]) + bvt_ref[...]
        cvt_ref[t] = vt.astype(BF16)


def _in_projection(h, w_all, b_all, wvt, bvt, bsz, seq):
    m = h.shape[0]
    steps_per_batch = seq // ROW_TILE
    tiles_per_step = ROW_TILE // TILE

    def rows(width, dtype):
        return (pl.BlockSpec((ROW_TILE, width), lambda i: (i, 0)),
                jax.ShapeDtypeStruct((m, width), dtype))

    outs = [rows(512, F32), rows(768, BF16), rows(512, BF16),
            (pl.BlockSpec((None, tiles_per_step, GROUP_W, TILE),
                          lambda i: (i // steps_per_batch, i % steps_per_batch, 0, 0)),
             jax.ShapeDtypeStruct((bsz, seq // TILE, GROUP_W, TILE), BF16)),
            rows(512, BF16), rows(256, BF16), rows(128, F32), rows(256, BF16), rows(1024, F32)]
    return pl.pallas_call(
        _inproj_kernel,
        grid=(m // ROW_TILE,),
        in_specs=[pl.BlockSpec((ROW_TILE, D_MODEL), lambda i: (i, 0)),
                  pl.BlockSpec((D_MODEL, N_PROJ), lambda i: (0, 0)),
                  pl.BlockSpec((1, N_PROJ), lambda i: (0, 0)),
                  pl.BlockSpec((GROUP_W, D_MODEL), lambda i: (0, 0)),
                  pl.BlockSpec((GROUP_W, 1), lambda i: (0, 0))],
        out_specs=[o[0] for o in outs],
        out_shape=[o[1] for o in outs],
        compiler_params=_cparams(("parallel",)),
        name="in_projection",
    )(h, w_all, b_all, wvt, bvt)


def _gmlp_kernel(uv_ref, gate_ref, lng_ref, lnb_ref, ws_ref, bs_ref, o_ref):
    lane_group = lax.broadcasted_iota(I32, (A_BLOCK, GROUP_W), 1) // HEAD_DIM
    for blk in range(ROW_TILE // A_BLOCK):
        r = slice(blk * A_BLOCK, (blk + 1) * A_BLOCK)
        u = uv_ref[r, 0:GROUP_W]
        v = _layer_norm(uv_ref[r, GROUP_W:2 * GROUP_W], lng_ref[...], lnb_ref[...])
        vb = v.astype(BF16)
        mixed = bs_ref[...]
        for g in range(N_HEADS):
            mg = jnp.dot(ws_ref[g], vb, preferred_element_type=F32)
            mixed = mixed + jnp.where(lane_group == g, mg, 0.0)
        o_ref[r, :] = ((u * mixed) * gate_ref[r, :]).astype(BF16)


def _gmlp(uv, gates, lng, lnb, ws_masked, bs_lanes):
    m = uv.shape[0]
    return pl.pallas_call(
        _gmlp_kernel,
        grid=(m // ROW_TILE,),
        in_specs=[pl.BlockSpec((ROW_TILE, 512), lambda i: (i, 0)),
                  pl.BlockSpec((ROW_TILE, GROUP_W), lambda i: (i, 0)),
                  pl.BlockSpec((1, GROUP_W), lambda i: (0, 0)),
                  pl.BlockSpec((1, GROUP_W), lambda i: (0, 0)),
                  pl.BlockSpec((N_HEADS, A_BLOCK, A_BLOCK), lambda i: (0, 0, 0)),
                  pl.BlockSpec((A_BLOCK, GROUP_W), lambda i: (0, 0))],
        out_specs=pl.BlockSpec((ROW_TILE, GROUP_W), lambda i: (i, 0)),
        out_shape=jax.ShapeDtypeStruct((m, GROUP_W), BF16),
        compiler_params=_cparams(("parallel",)),
        name="gmlp",
    )(uv, gates, lng, lnb, ws_masked, bs_lanes)


def _band_kernel(qkv_ref, gate_ref, bias_ref, o_ref):
    i = pl.program_id(1)
    q = qkv_ref[i, :, 0:GROUP_W]
    lane_head = lax.broadcasted_iota(I32, (TILE, GROUP_W), 1) // HEAD_DIM
    ks, vs, live = [], [], []
    for d in (2, 1, 0):
        kb = i - d
        kbc = jnp.maximum(kb, 0)
        ks.append(qkv_ref[kbc, :, GROUP_W:2 * GROUP_W])
        vs.append(qkv_ref[kbc, :, 2 * GROUP_W:3 * GROUP_W])
        live.append(kb >= 0)
    out = jnp.zeros((TILE, GROUP_W), F32)
    for h in range(N_HEADS):
        qh = jnp.where(lane_head == h, q, jnp.zeros_like(q))
        ss = []
        for t in range(3):
            s = _nt_dot(qh, ks[t]) + bias_ref[h, :, t * TILE:(t + 1) * TILE]
            ss.append(jnp.where(live[t], s, NEG_INF))
        mx = jnp.maximum(jnp.maximum(jnp.max(ss[0], axis=1, keepdims=True),
                                     jnp.max(ss[1], axis=1, keepdims=True)),
                         jnp.max(ss[2], axis=1, keepdims=True))
        den = jnp.zeros((TILE, 1), F32)
        pv = jnp.zeros((TILE, GROUP_W), F32)
        for t in range(3):
            p = jnp.exp(ss[t] - mx)
            den = den + jnp.sum(p, axis=1, keepdims=True)
            pv = pv + jnp.dot(p.astype(BF16), vs[t], preferred_element_type=F32)
        out = jnp.where(lane_head == h, pv * (1.0 / den), out)
    o_ref[...] = (out * gate_ref[...]).astype(BF16)


def _band_attention(bqkv4, gates, bias, bsz, seq):
    nt = seq // TILE
    return pl.pallas_call(
        _band_kernel,
        grid=(bsz, nt),
        in_specs=[pl.BlockSpec((None, nt, TILE, 768), lambda b, i: (b, 0, 0, 0)),
                  pl.BlockSpec((TILE, GROUP_W), lambda b, i: (b * nt + i, 1)),
                  pl.BlockSpec((N_HEADS, TILE, 3 * TILE), lambda b, i: (0, 0, 0))],
        out_specs=pl.BlockSpec((TILE, GROUP_W), lambda b, i: (b * nt + i, 0)),
        out_shape=jax.ShapeDtypeStruct((bsz * seq, GROUP_W), BF16),
        compiler_params=_cparams(("parallel", "arbitrary")),
        name="band_attention",
    )(bqkv4, gates, bias)


def _memkv_kernel(mem_ref, w_ref, km_ref, vm_ref):
    kv = jnp.dot(mem_ref[...].astype(BF16), w_ref[...], preferred_element_type=F32)
    km_ref[...] = kv[:, 0:GROUP_W].astype(BF16)
    vm_ref[...] = kv[:, GROUP_W:2 * GROUP_W].astype(BF16)


def _memory_kv(mem, w_kv):
    bsz, n_mem, _ = mem.shape
    blk = pl.BlockSpec((None, n_mem, GROUP_W), lambda b: (b, 0, 0))
    shp = jax.ShapeDtypeStruct((bsz, n_mem, GROUP_W), BF16)
    return pl.pallas_call(
        _memkv_kernel,
        grid=(bsz,),
        in_specs=[pl.BlockSpec((None, n_mem, D_MODEL), lambda b: (b, 0, 0)),
                  pl.BlockSpec((D_MODEL, 2 * GROUP_W), lambda b: (0, 0))],
        out_specs=[blk, blk],
        out_shape=[shp, shp],
        compiler_params=_cparams(("parallel",)),
        name="memory_kv",
    )(mem, w_kv)


def _memattn_kernel(q_ref, km_ref, vm_ref, gate_ref, o_ref):
    q = q_ref[...]
    lane_head = lax.broadcasted_iota(I32, (ROW_TILE, GROUP_W), 1) // HEAD_DIM
    out = jnp.zeros((ROW_TILE, GROUP_W), F32)
    for h in range(N_HEADS):
        qh = jnp.where(lane_head == h, q, jnp.zeros_like(q))
        s = _nt_dot(qh, km_ref[...])
        p = jnp.exp(s - jnp.max(s, axis=1, keepdims=True))
        den = jnp.sum(p, axis=1, keepdims=True)
        pv = jnp.dot(p.astype(BF16), vm_ref[...], preferred_element_type=F32)
        out = jnp.where(lane_head == h, pv * (1.0 / den), out)
    o_ref[...] = (out * gate_ref[...]).astype(BF16)


def _memory_attention(mq, km, vm, gates, bsz, seq):
    steps = seq // ROW_TILE
    n_mem = km.shape[1]
    return pl.pallas_call(
        _memattn_kernel,
        grid=(bsz, steps),
        in_specs=[pl.BlockSpec((ROW_TILE, GROUP_W), lambda b, i: (b * steps + i, 0)),
                  pl.BlockSpec((None, n_mem, GROUP_W), lambda b, i: (b, 0, 0)),
                  pl.BlockSpec((None, n_mem, GROUP_W), lambda b, i: (b, 0, 0)),
                  pl.BlockSpec((ROW_TILE, GROUP_W), lambda b, i: (b * steps + i, 3))],
        out_specs=pl.BlockSpec((ROW_TILE, GROUP_W), lambda b, i: (b * steps + i, 0)),
        out_shape=jax.ShapeDtypeStruct((bsz * seq, GROUP_W), BF16),
        compiler_params=_cparams(("parallel", "arbitrary")),
        name="memory_attention",
    )(mq, km, vm, gates)


def _dsa_kernel(cqk_ref, cvt_ref, ik_ref, iq_ref, iw_ref, gate_ref, t5_ref, tri_ref, o_ref,
                key_ref, planes_ref, live_ref, am_ref, iwt_ref, qh_ref, lg_ref, sa_ref, sb_ref, pa_ref, pb_ref,
                m_ref, l_ref, alpha_ref, acc_ref):
    n = pl.program_id(1)
    nt = key_ref.shape[0]
    q0 = n * TILE
    n_pairs = n // 2 + 1
    score_scale = (IDX_DIM ** -0.5) * (IDX_HEADS ** -0.5)
    chunks = [slice(c * ROW_CHUNK, (c + 1) * ROW_CHUNK) for c in range(TILE // ROW_CHUNK)]

    q_limit = ((q0 + lax.broadcasted_iota(I32, (1, TILE), 1)) // CHUNK + 1) * CHUNK
    chunk_pos = lax.broadcasted_iota(I32, (ROW_CHUNK, 1), 0)

    def admissible(rows):
        return q0 + rows.start + chunk_pos < q_limit

    iwt_ref[...] = jnp.transpose(iw_ref[...])[0:IDX_HEADS, :] * score_scale

    def score_tile(j, diagonal):
        for p in range(IDX_HEADS // 2):
            for parity in range(2):
                lg_ref[2 * p + parity] = _nt_dot(ik_ref[j, :, parity * 128:(parity + 1) * 128],
                                                 iq_ref[:, p * 128:(p + 1) * 128])
        for rows in chunks:
            sc = None
            for h in range(IDX_HEADS):
                term = jnp.maximum(lg_ref[h, rows, :], 0.0) * iwt_ref[h:h + 1, :]
                sc = term if sc is None else sc + term
            if diagonal:
                sc = jnp.where(admissible(rows), sc, NEG_INF)
            bits = pltpu.bitcast(sc, I32)
            sign = lax.shift_right_arithmetic(bits, 31)
            key_ref[j, rows, :] = (bits ^ (sign & 0x7FFFFFFF)) - sign
        for lanes in (slice(0, 128), slice(128, 256)):
            x = [key_ref[j, SUBLANES * i:SUBLANES * (i + 1), lanes] for i in range(32)]
            step, keep = 16, 0x0000FFFF
            while step:
                k = 0
                while k < 32:
                    t = (x[k] ^ lax.shift_right_logical(x[k + step], step)) & keep
                    x[k] = x[k] ^ t
                    x[k + step] = x[k + step] ^ lax.shift_left(t, step)
                    k = (k + step + 1) & ~step
                step >>= 1
                keep = (keep ^ (keep << step)) & 0xFFFFFFFF
            dst = pl.ds(pl.multiple_of(j * SUBLANES, SUBLANES), SUBLANES)
            for b in range(32):
                plane = x[31 - b]
                planes_ref[b, dst, lanes] = ~plane if b == 31 else plane

    def far_score_tile(j, carry):
        score_tile(j, False)
        return carry

    lax.fori_loop(0, n, far_score_tile, 0)
    score_tile(n, True)

    n_groups = n // GROUP_TILES + 1
    group_rows = GROUP_TILES * SUBLANES

    def clear_tail(t, carry):
        dst = pl.ds(pl.multiple_of(t * SUBLANES, SUBLANES), SUBLANES)
        planes_ref[:, dst, :] = jnp.zeros((32, SUBLANES, TILE), I32)
        live_ref[dst, :] = jnp.zeros((SUBLANES, TILE), I32)
        return carry

    lax.fori_loop(n + 1, n_groups * GROUP_TILES, clear_tail, 0)

    def mark_live(t, carry):
        live_ref[pl.ds(pl.multiple_of(t * SUBLANES, SUBLANES), SUBLANES), :] = jnp.full((SUBLANES, TILE), -1, I32)
        return carry

    lax.fori_loop(0, n + 1, mark_live, 0)

    def group(g):
        return pl.ds(pl.multiple_of(g * group_rows, group_rows), group_rows)

    def count_live(b):
        def body(g, c):
            ones = live_ref[group(g), :]
            if b is not None:
                ones = ones & planes_ref[b, group(g), :]
            pc = lax.population_count(ones)
            for r in range(GROUP_TILES):
                c = c + pc[r * SUBLANES:(r + 1) * SUBLANES, :]
            return c
        c = lax.fori_loop(0, n_groups, body, jnp.zeros((SUBLANES, TILE), I32))
        return jnp.sum(c, axis=0, keepdims=True)

    def bit_step(t, carry):
        need, thr_u = carry
        b = 31 - t
        total = count_live(b)
        take = jnp.where(total >= need, -1, 0)
        need = jnp.where(total >= need, need, need - total)
        thr_u = thr_u | (take & lax.shift_left(jnp.int32(1), b))

        def narrow(g, c):
            live = live_ref[group(g), :]
            ones = live & planes_ref[b, group(g), :]
            live_ref[group(g), :] = (live ^ ones) ^ (live & take)
            return c

        lax.fori_loop(0, n_groups, narrow, 0)
        return need, thr_u

    quota, thr_u = lax.fori_loop(0, 32, bit_step,
                                 (jnp.full((1, TILE), TOPK, I32), jnp.zeros((1, TILE), I32)))
    thr = thr_u ^ SIGN_BIT
    n_equal = count_live(None)
    ties_cut = jnp.max(n_equal - quota) > 0

    am_ref[nt + 2 * N_HEADS] = jnp.full((TILE, TILE), NEG_INF, F32)

    def write_masks(with_ties):
        quota_f = quota.astype(F32)

        def selected(j, rows, taken, diagonal):
            k = key_ref[j, rows, :]
            open_ = jnp.where(admissible(rows), 0.0, NEG_INF) if diagonal else 0.0
            if not with_ties:
                return jnp.where(k >= thr, open_, NEG_INF), taken
            eq = jnp.where(k == thr, 1.0, 0.0).astype(BF16)
            before = jnp.dot(tri_ref[rows.start:rows.stop, rows.start:rows.stop], eq,
                             preferred_element_type=F32) + taken
            tied = jnp.where(k == thr, jnp.where(before < quota_f, open_, NEG_INF), NEG_INF)
            return jnp.where(k > thr, open_, tied), taken + jnp.sum(eq.astype(F32), axis=0, keepdims=True)

        def far_mask(j, taken):
            for rows in chunks:
                am, taken = selected(j, rows, taken, False)
                am_ref[j, rows, :] = am
            return taken

        taken = lax.fori_loop(0, n - 1, far_mask, jnp.zeros((1, TILE), F32))

        def near_mask(j, d, taken):
            for rows in chunks:
                am, taken = selected(j, rows, taken, d == 1)
                for h in range(N_HEADS):
                    am_ref[nt + N_HEADS * d + h, rows, :] = am + t5_ref[h, d * TILE + rows.start:d * TILE + rows.stop, :]
            return taken

        prev = near_mask(jnp.maximum(n - 1, 0), 0, taken)
        near_mask(n, 1, jnp.where(n >= 1, prev, taken))

    @pl.when(ties_cut)
    def _():
        write_masks(True)

    @pl.when(jnp.logical_not(ties_cut))
    def _():
        write_masks(False)

    m_ref[...] = jnp.full(m_ref.shape, -1e30, F32)
    l_ref[...] = jnp.zeros(l_ref.shape, F32)
    acc_ref[...] = jnp.zeros(acc_ref.shape, F32)
    q = cqk_ref[n, :, 0:GROUP_W]
    lane_head = lax.broadcasted_iota(I32, (TILE, GROUP_W), 1) // HEAD_DIM
    for h in range(N_HEADS):
        qh_ref[h] = jnp.where(lane_head == h, q, jnp.zeros_like(q))

    def logits(j, s_buf):
        k = cqk_ref[jnp.minimum(j, n), :, GROUP_W:2 * GROUP_W]
        for h in range(N_HEADS):
            s_buf[h] = _nt_dot(k, qh_ref[h])

    def softmax_step(j, s_buf, p_buf):
        near = j - (n - 1)
        for h in range(N_HEADS):
            slab = jnp.where(j > n, nt + 2 * N_HEADS, jnp.where(near >= 0, nt + N_HEADS * near + h, j))
            m_old = m_ref[h]
            mx = None
            for rows in chunks:
                s = s_buf[h, rows, :] + am_ref[slab, rows, :]
                s_buf[h, rows, :] = s
                mx = s if mx is None else jnp.maximum(mx, s)
            m_new = jnp.maximum(m_old, jnp.max(mx, axis=0, keepdims=True))
            alpha = jnp.exp2(m_old - m_new)
            tot = None
            for rows in chunks:
                p = jnp.exp2(s_buf[h, rows, :] - m_new)
                p_buf[h, rows, :] = p.astype(BF16)
                tot = p if tot is None else tot + p
            l_ref[h] = alpha * l_ref[h] + jnp.sum(tot, axis=0, keepdims=True)
            m_ref[h] = m_new
            alpha_ref[h] = alpha

    def accumulate(j, p_buf):
        jc = jnp.minimum(j, n)
        for h in range(N_HEADS):
            vt = cvt_ref[jc, h * HEAD_DIM:(h + 1) * HEAD_DIM, :]
            acc_ref[h] = alpha_ref[h] * acc_ref[h] + jnp.dot(vt, p_buf[h], preferred_element_type=F32)

    logits(0, sa_ref)

    def tile_pair(i, carry):
        j = 2 * i
        logits(j + 1, sb_ref)
        softmax_step(j, sa_ref, pa_ref)
        accumulate(j, pa_ref)
        logits(j + 2, sa_ref)
        softmax_step(j + 1, sb_ref, pb_ref)
        accumulate(j + 1, pb_ref)
        return carry

    lax.fori_loop(0, n_pairs, tile_pair, 0)

    out_t = jnp.concatenate([acc_ref[h] * (1.0 / l_ref[h]) for h in range(N_HEADS)], axis=0)
    o_ref[...] = (jnp.transpose(out_t) * gate_ref[...]).astype(BF16)


def _dsa_attention(cqk4, cvt, ik4, iq, iw, gates, t5_rel, tri, bsz, seq):
    nt = seq // TILE
    assert nt % 2 == 0
    return pl.pallas_call(
        _dsa_kernel,
        grid=(bsz, nt),
        in_specs=[pl.BlockSpec((None, nt, TILE, 512), lambda b, i: (b, 0, 0, 0)),
                  pl.BlockSpec((None, nt, GROUP_W, TILE), lambda b, i: (b, 0, 0, 0)),
                  pl.BlockSpec((None, nt, TILE, 256), lambda b, i: (b, 0, 0, 0)),
                  pl.BlockSpec((TILE, 512), lambda b, i: (b * nt + i, 0)),
                  pl.BlockSpec((TILE, 128), lambda b, i: (b * nt + i, 0)),
                  pl.BlockSpec((TILE, GROUP_W), lambda b, i: (b * nt + i, 2)),
                  pl.BlockSpec((N_HEADS, 2 * TILE, TILE), lambda b, i: (0, 0, 0)),
                  pl.BlockSpec((TILE, TILE), lambda b, i: (0, 0))],
        out_specs=pl.BlockSpec((TILE, GROUP_W), lambda b, i: (b * nt + i, 0)),
        out_shape=jax.ShapeDtypeStruct((bsz * seq, GROUP_W), BF16),
        scratch_shapes=[pltpu.VMEM((nt, TILE, TILE), I32),
                        pltpu.VMEM((32, nt * SUBLANES, TILE), I32),
                        pltpu.VMEM((nt * SUBLANES, TILE), I32),
                        pltpu.VMEM((nt + 2 * N_HEADS + 1, TILE, TILE), F32),
                        pltpu.VMEM((IDX_HEADS, TILE), F32),
                        pltpu.VMEM((N_HEADS, TILE, GROUP_W), BF16),
                        pltpu.VMEM((IDX_HEADS, TILE, TILE), F32),
                        pltpu.VMEM((N_HEADS, TILE, TILE), F32),
                        pltpu.VMEM((N_HEADS, TILE, TILE), F32),
                        pltpu.VMEM((N_HEADS, TILE, TILE), BF16),
                        pltpu.VMEM((N_HEADS, TILE, TILE), BF16),
                        pltpu.VMEM((N_HEADS, 1, TILE), F32),
                        pltpu.VMEM((N_HEADS, 1, TILE), F32),
                        pltpu.VMEM((N_HEADS, 1, TILE), F32),
                        pltpu.VMEM((N_HEADS, HEAD_DIM, TILE), F32)],
        compiler_params=_cparams(("parallel", "arbitrary")),
        name="dsa_attention",
    )(cqk4, cvt, ik4, iq, iw, gates, t5_rel, tri)


def _outproj_kernel(h_ref, ya_ref, yb_ref, yc_ref, ym_ref, w_ref, b_ref, g_ref, beta_ref, o_ref):
    y = b_ref[...]
    for idx, part in enumerate((ya_ref, yb_ref, yc_ref, ym_ref)):
        y = y + jnp.dot(part[...], w_ref[idx * GROUP_W:(idx + 1) * GROUP_W, :],
                        preferred_element_type=F32)
    o_ref[...] = _layer_norm(DEEPNORM_ALPHA * h_ref[...] + y, g_ref[...], beta_ref[...])


def _out_projection(h, ya, yb, yc, ym, w_out, b_out, g, beta):
    m = h.shape[0]
    part = pl.BlockSpec((ROW_TILE, GROUP_W), lambda i: (i, 0))
    vec = pl.BlockSpec((1, D_MODEL), lambda i: (0, 0))
    return pl.pallas_call(
        _outproj_kernel,
        grid=(m // ROW_TILE,),
        in_specs=[pl.BlockSpec((ROW_TILE, D_MODEL), lambda i: (i, 0)), part, part, part, part,
                  pl.BlockSpec((D_MODEL, D_MODEL), lambda i: (0, 0)), vec, vec, vec],
        out_specs=pl.BlockSpec((ROW_TILE, D_MODEL), lambda i: (i, 0)),
        out_shape=jax.ShapeDtypeStruct((m, D_MODEL), F32),
        compiler_params=_cparams(("parallel",)),
        name="out_projection",
    )(h, ya, yb, yc, ym, w_out, b_out, g, beta)


def _prep_in_weights(w_in, b_in):
    sp = np.cumsum((0, 256, 256, 256, 256, 256, 256, 256, 256, 256, 256, 256, 512, 64, 8, 256, 256))
    names = ("a_u", "a_v", "a_g", "bq", "bk", "bv", "bg", "cq", "ck", "cv", "cg", "iq", "ik", "iw", "mq", "mg")
    col = {nm: slice(int(sp[i]), int(sp[i + 1])) for i, nm in enumerate(names)}
    qs = HEAD_DIM ** -0.5

    def build(t):
        z = lambda width: jnp.zeros((t.shape[0], width), t.dtype)
        ik = t[:, col["ik"]]
        parts = [t[:, col["a_u"]], t[:, col["a_v"]],
                 t[:, col["bq"]] * qs, t[:, col["bk"]], t[:, col["bv"]],
                 t[:, col["cq"]] * (qs * LOG2E), t[:, col["ck"]],
                 t[:, col["iq"]],
                 ik, z(64), z(64), ik,
                 t[:, col["iw"]], z(120),
                 t[:, col["mq"]] * qs,
                 t[:, col["a_g"]], t[:, col["bg"]], t[:, col["cg"]], t[:, col["mg"]]]
        return jnp.concatenate(parts, axis=1), t[:, col["cv"]]

    w_all, w_cv = build(w_in)
    b_all, b_cv = build(b_in.reshape(1, -1))
    return (w_all.astype(BF16), b_all, jnp.transpose(w_cv).astype(BF16), jnp.transpose(b_cv))


def _toeplitz(g, n_rows, n_cols):
    span = n_rows + n_cols - 1
    lead = g.shape[:-1]
    period = jnp.concatenate([g, jnp.zeros(lead + (1,), g.dtype)], axis=-1)
    flat = jnp.tile(period, (1,) * len(lead) + (n_rows,))[..., :n_rows * span]
    return flat.reshape(lead + (n_rows, span))[..., n_rows - 1:n_rows - 1 + n_cols]


def _band_bias_table(b_rel):
    m = np.arange(4 * TILE - 1)
    rel_idx = np.clip(3 * TILE - 1 - m, -REL_CLIP, REL_CLIP) + REL_CLIP
    bias = _toeplitz(b_rel[:, jnp.asarray(rel_idx)], TILE, 3 * TILE)
    i = np.arange(TILE)[:, None]
    j = np.arange(3 * TILE)[None, :]
    chunk_diff = i // CHUNK - j // CHUNK + B_PREV_CHUNKS
    allowed = (chunk_diff >= 0) & (chunk_diff <= B_PREV_CHUNKS)
    return jnp.where(jnp.asarray(allowed)[None], bias, NEG_INF).astype(F32)


def _t5_bucket(rel):
    nb = T5_BUCKETS // 2
    max_exact = nb // 2
    ret = jnp.where(rel > 0, nb, 0)
    n = jnp.abs(rel)
    nf = jnp.maximum(n, 1).astype(F32)
    large = max_exact + (jnp.log(nf / max_exact) / math.log(T5_MAX_DIST / max_exact)
                         * (nb - max_exact)).astype(I32)
    large = jnp.minimum(large, nb - 1)
    return ret + jnp.where(n < max_exact, n, large)


def _t5_rel_table(t5_table):
    m = jnp.arange(3 * TILE - 1, dtype=I32)
    near = t5_table[_t5_bucket(TILE - 1 - m)]
    far = t5_table[_t5_bucket(jnp.full((1,), -(2 * TILE + 1), I32))]
    return _toeplitz(jnp.transpose(near - far) * LOG2E, 2 * TILE, TILE).astype(F32)


def kernel(x, mem, ln_in_g, ln_in_b, w_in, b_in, a_ln_g, a_ln_b, a_ws, a_bs, b_rel, t5_table,
           w_mem_kv, w_out, b_out, ln_g, ln_b):
    bsz, seq, _ = x.shape
    assert seq % ROW_TILE == 0 and x.shape[2] == D_MODEL
    m = bsz * seq
    nt = seq // TILE

    cpos = np.arange(A_BLOCK) // CHUNK
    a_mask = jnp.asarray(cpos[None, :] <= cpos[:, None])
    t5_rel = _t5_rel_table(t5_table)
    tri = jnp.asarray(np.tril(np.ones((TILE, TILE), np.float32), -1)).astype(BF16)

    h = _input_norm(x.reshape(m, D_MODEL), ln_in_g, ln_in_b)
    for l in range(DEPTH):
        w_all, b_all, wvt, bvt = _prep_in_weights(w_in[l], b_in[l])
        uv, bqkv, cqk, cvt, iq, ik, iw, mq, gates = _in_projection(h, w_all, b_all, wvt, bvt, bsz, seq)

        ws_masked = jnp.where(a_mask[None], a_ws[l], 0.0).astype(BF16)
        bs_lanes = jnp.repeat(jnp.transpose(a_bs[l]), HEAD_DIM, axis=1)
        ya = _gmlp(uv, gates, a_ln_g[l].reshape(1, -1), a_ln_b[l].reshape(1, -1), ws_masked, bs_lanes)

        yb = _band_attention(bqkv.reshape(bsz, nt, TILE, 768), gates, _band_bias_table(b_rel[l]), bsz, seq)

        yc = _dsa_attention(cqk.reshape(bsz, nt, TILE, 512), cvt, ik.reshape(bsz, nt, TILE, 256),
                            iq, iw, gates, t5_rel, tri, bsz, seq)

        km, vm = _memory_kv(mem, w_mem_kv[l].astype(BF16))
        ym = _memory_attention(mq, km, vm, gates, bsz, seq)

        h = _out_projection(h, ya, yb, yc, ym, w_out[l].astype(BF16), b_out[l].reshape(1, -1),
                            ln_g[l].reshape(1, -1), ln_b[l].reshape(1, -1))
    return h.reshape(bsz, seq, D_MODEL)
```

```python
import functools
import math

import numpy as np
import jax
import jax.numpy as jnp
from jax import lax
from jax.experimental import pallas as pl
from jax.experimental.pallas import tpu as pltpu

F32 = jnp.float32
BF16 = jnp.bfloat16
I32 = jnp.int32

D_MODEL = 1024
CHUNK = 64
HEAD_DIM = 64
GROUP_W = 256
N_HEADS = GROUP_W // HEAD_DIM
A_BLOCK = 128
B_PREV_CHUNKS = 8
REL_CLIP = 128
IDX_HEADS = 8
IDX_DIM = 64
TOPK = 256
T5_BUCKETS = 32
T5_MAX_DIST = 128
N_MEM = 256
DEPTH = 2
DEEPNORM_ALPHA = (2 * DEPTH) ** 0.25
LN_EPS = 1e-5

TILE = 256
ROW_TILE = 512
VMEM_LIMIT = 56 * 1024 * 1024
NEG_INF = float("-inf")
SIGN_BIT = -2 ** 31
SUBLANES = 8
GROUP_TILES = 8
V_ROWS = HEAD_DIM + 16
ROW_CHUNK = 64
LOG2E = 1.4426950408889634

C_UV = (0, 512)
C_BQKV = (512, 1280)
C_CQK = (1280, 1792)
C_MQ = (1792, 2048)
C_GATE = (2048, 3072)
N_MAIN = 3072
C_IQ = (0, 512)
C_IK = (512, 768)
C_IW = (768, 896)
N_IDX = 896


def _cparams(sem):
    return pltpu.CompilerParams(dimension_semantics=sem, vmem_limit_bytes=VMEM_LIMIT)


def _resident(block_shape, index_map):
    return pl.BlockSpec(block_shape, index_map, pipeline_mode=pl.Buffered(1))


def _nt_dot(a, b):
    return lax.dot_general(a, b, (((1,), (1,)), ((), ())), preferred_element_type=F32)


def _gelu_tanh(x):
    return 0.5 * x * (1.0 + jnp.tanh(math.sqrt(2.0 / math.pi) * (x + 0.044715 * (x * x * x))))


def _silu(x):
    return x * (1.0 / (1.0 + jnp.exp(-x)))


def _layer_norm(x, g, b):
    mu = jnp.mean(x, axis=-1, keepdims=True)
    xc = x - mu
    var = jnp.mean(xc * xc, axis=-1, keepdims=True)
    return xc * lax.rsqrt(var + LN_EPS) * g + b


def _ln_kernel(x_ref, g_ref, b_ref, o_ref):
    o_ref[...] = _layer_norm(x_ref[...], g_ref[...], b_ref[...])


def _input_norm(x2, g, b):
    m = x2.shape[0]
    return pl.pallas_call(
        _ln_kernel,
        grid=(m // ROW_TILE,),
        in_specs=[pl.BlockSpec((ROW_TILE, D_MODEL), lambda i: (i, 0)),
                  pl.BlockSpec((1, D_MODEL), lambda i: (0, 0)),
                  pl.BlockSpec((1, D_MODEL), lambda i: (0, 0))],
        out_specs=pl.BlockSpec((ROW_TILE, D_MODEL), lambda i: (i, 0)),
        out_shape=jax.ShapeDtypeStruct((m, D_MODEL), F32),
        compiler_params=_cparams(("parallel",)),
        name="input_norm",
    )(x2, g.reshape(1, D_MODEL), b.reshape(1, D_MODEL))


def _inproj_kernel(h_ref, w_ref, b_ref, widx_ref, bidx_ref, wvt_ref, bvt_ref,
                   uv_ref, bqkv_ref, cqk_ref, cvt_ref, iq_ref, ik_ref, iw_ref, mq_ref, gate_ref):
    x = h_ref[...]
    xm = x.astype(w_ref.dtype)

    def seg(c):
        return jnp.dot(xm, w_ref[:, c[0]:c[1]], preferred_element_type=F32) + b_ref[:, c[0]:c[1]]

    def idx_seg(c):
        return jnp.dot(x, widx_ref[:, c[0]:c[1]], preferred_element_type=F32) + bidx_ref[:, c[0]:c[1]]

    uv_ref[...] = _gelu_tanh(seg(C_UV))
    bqkv_ref[...] = seg(C_BQKV).astype(bqkv_ref.dtype)
    cqk_ref[...] = seg(C_CQK).astype(cqk_ref.dtype)
    mq_ref[...] = seg(C_MQ).astype(mq_ref.dtype)
    gate_ref[...] = _silu(seg(C_GATE))
    iq_ref[...] = idx_seg(C_IQ)
    ik_ref[...] = idx_seg(C_IK)
    iw_ref[...] = idx_seg(C_IW)
    for t in range(ROW_TILE // TILE):
        vt = (_nt_dot(wvt_ref[...], xm[t * TILE:(t + 1) * TILE, :]) + bvt_ref[...]).astype(cvt_ref.dtype)
        for h in range(N_HEADS):
            cvt_ref[t, h * V_ROWS:h * V_ROWS + HEAD_DIM, :] = vt[h * HEAD_DIM:(h + 1) * HEAD_DIM, :]
            cvt_ref[t, h * V_ROWS + HEAD_DIM:(h + 1) * V_ROWS, :] = jnp.ones((V_ROWS - HEAD_DIM, TILE), cvt_ref.dtype)


def _in_projection(h, w_main, b_main, w_idx, b_idx, wvt, bvt, bsz, seq):
    m = h.shape[0]
    sdt = w_main.dtype
    steps_per_batch = seq // ROW_TILE
    tiles_per_step = ROW_TILE // TILE

    def rows(width, dtype):
        return (pl.BlockSpec((ROW_TILE, width), lambda i: (i, 0)),
                jax.ShapeDtypeStruct((m, width), dtype))

    outs = [rows(512, F32), rows(768, sdt), rows(512, sdt),
            (pl.BlockSpec((None, tiles_per_step, N_HEADS * V_ROWS, TILE),
                          lambda i: (i // steps_per_batch, i % steps_per_batch, 0, 0)),
             jax.ShapeDtypeStruct((bsz, seq // TILE, N_HEADS * V_ROWS, TILE), sdt)),
            rows(512, F32), rows(256, F32), rows(128, F32), rows(256, sdt), rows(1024, F32)]
    return pl.pallas_call(
        _inproj_kernel,
        grid=(m // ROW_TILE,),
        in_specs=[pl.BlockSpec((ROW_TILE, D_MODEL), lambda i: (i, 0)),
                  _resident((D_MODEL, N_MAIN), lambda i: (0, 0)),
                  pl.BlockSpec((1, N_MAIN), lambda i: (0, 0)),
                  _resident((D_MODEL, N_IDX), lambda i: (0, 0)),
                  pl.BlockSpec((1, N_IDX), lambda i: (0, 0)),
                  _resident((GROUP_W, D_MODEL), lambda i: (0, 0)),
                  pl.BlockSpec((GROUP_W, 1), lambda i: (0, 0))],
        out_specs=[o[0] for o in outs],
        out_shape=[o[1] for o in outs],
        compiler_params=_cparams(("parallel",)),
        name="in_projection",
    )(h, w_main, b_main, w_idx, b_idx, wvt, bvt)


def _gmlp_kernel(uv_ref, gate_ref, lng_ref, lnb_ref, ws_ref, bs_ref, o_ref):
    lane_group = lax.broadcasted_iota(I32, (A_BLOCK, GROUP_W), 1) // HEAD_DIM
    for blk in range(ROW_TILE // A_BLOCK):
        r = slice(blk * A_BLOCK, (blk + 1) * A_BLOCK)
        u = uv_ref[r, 0:GROUP_W]
        v = _layer_norm(uv_ref[r, GROUP_W:2 * GROUP_W], lng_ref[...], lnb_ref[...])
        vb = v.astype(ws_ref.dtype)
        mixed = bs_ref[...]
        for g in range(N_HEADS):
            mg = jnp.dot(ws_ref[g], vb, preferred_element_type=F32)
            mixed = mixed + jnp.where(lane_group == g, mg, 0.0)
        o_ref[r, :] = ((u * mixed) * gate_ref[r, :]).astype(o_ref.dtype)


def _gmlp(uv, gates, lng, lnb, ws_masked, bs_lanes):
    m = uv.shape[0]
    return pl.pallas_call(
        _gmlp_kernel,
        grid=(m // ROW_TILE,),
        in_specs=[pl.BlockSpec((ROW_TILE, 512), lambda i: (i, 0)),
                  pl.BlockSpec((ROW_TILE, GROUP_W), lambda i: (i, 0)),
                  pl.BlockSpec((1, GROUP_W), lambda i: (0, 0)),
                  pl.BlockSpec((1, GROUP_W), lambda i: (0, 0)),
                  pl.BlockSpec((N_HEADS, A_BLOCK, A_BLOCK), lambda i: (0, 0, 0)),
                  pl.BlockSpec((A_BLOCK, GROUP_W), lambda i: (0, 0))],
        out_specs=pl.BlockSpec((ROW_TILE, GROUP_W), lambda i: (i, 0)),
        out_shape=jax.ShapeDtypeStruct((m, GROUP_W), ws_masked.dtype),
        compiler_params=_cparams(("parallel",)),
        name="gmlp",
    )(uv, gates, lng, lnb, ws_masked, bs_lanes)


def _band_kernel(qkv_ref, gate_ref, bias_ref, o_ref):
    i = pl.program_id(1)
    q = qkv_ref[i, :, 0:GROUP_W]
    lane_head = lax.broadcasted_iota(I32, (TILE, GROUP_W), 1) // HEAD_DIM
    ks, vs, live = [], [], []
    for d in (2, 1, 0):
        kb = i - d
        kbc = jnp.maximum(kb, 0)
        ks.append(qkv_ref[kbc, :, GROUP_W:2 * GROUP_W])
        vs.append(qkv_ref[kbc, :, 2 * GROUP_W:3 * GROUP_W])
        live.append(kb >= 0)
    out = jnp.zeros((TILE, GROUP_W), F32)
    for h in range(N_HEADS):
        qh = jnp.where(lane_head == h, q, jnp.zeros_like(q))
        ss = []
        for t in range(3):
            s = _nt_dot(qh, ks[t]) + bias_ref[h, :, t * TILE:(t + 1) * TILE]
            ss.append(jnp.where(live[t], s, NEG_INF))
        mx = jnp.maximum(jnp.maximum(jnp.max(ss[0], axis=1, keepdims=True),
                                     jnp.max(ss[1], axis=1, keepdims=True)),
                         jnp.max(ss[2], axis=1, keepdims=True))
        den = jnp.zeros((TILE, 1), F32)
        pv = jnp.zeros((TILE, GROUP_W), F32)
        for t in range(3):
            p = jnp.exp(ss[t] - mx)
            den = den + jnp.sum(p, axis=1, keepdims=True)
            pv = pv + jnp.dot(p.astype(vs[t].dtype), vs[t], preferred_element_type=F32)
        out = jnp.where(lane_head == h, pv * (1.0 / den), out)
    o_ref[...] = (out * gate_ref[...]).astype(o_ref.dtype)


def _band_attention(bqkv4, gates, bias, bsz, seq):
    nt = seq // TILE
    return pl.pallas_call(
        _band_kernel,
        grid=(bsz, nt),
        in_specs=[pl.BlockSpec((None, nt, TILE, 768), lambda b, i: (b, 0, 0, 0)),
                  pl.BlockSpec((TILE, GROUP_W), lambda b, i: (b * nt + i, 1)),
                  pl.BlockSpec((N_HEADS, TILE, 3 * TILE), lambda b, i: (0, 0, 0))],
        out_specs=pl.BlockSpec((TILE, GROUP_W), lambda b, i: (b * nt + i, 0)),
        out_shape=jax.ShapeDtypeStruct((bsz * seq, GROUP_W), bqkv4.dtype),
        compiler_params=_cparams(("parallel", "arbitrary")),
        name="band_attention",
    )(bqkv4, gates, bias)


def _memkv_kernel(mem_ref, w_ref, km_ref, vm_ref):
    kv = jnp.dot(mem_ref[...].astype(w_ref.dtype), w_ref[...], preferred_element_type=F32)
    km_ref[...] = kv[:, 0:GROUP_W].astype(km_ref.dtype)
    vm_ref[...] = kv[:, GROUP_W:2 * GROUP_W].astype(vm_ref.dtype)


def _memory_kv(mem, w_kv):
    bsz, n_mem, _ = mem.shape
    blk = pl.BlockSpec((None, n_mem, GROUP_W), lambda b: (b, 0, 0))
    shp = jax.ShapeDtypeStruct((bsz, n_mem, GROUP_W), w_kv.dtype)
    return pl.pallas_call(
        _memkv_kernel,
        grid=(bsz,),
        in_specs=[pl.BlockSpec((None, n_mem, D_MODEL), lambda b: (b, 0, 0)),
                  pl.BlockSpec((D_MODEL, 2 * GROUP_W), lambda b: (0, 0))],
        out_specs=[blk, blk],
        out_shape=[shp, shp],
        compiler_params=_cparams(("parallel",)),
        name="memory_kv",
    )(mem, w_kv)


def _memattn_kernel(q_ref, km_ref, vm_ref, gate_ref, o_ref):
    q = q_ref[...]
    lane_head = lax.broadcasted_iota(I32, (ROW_TILE, GROUP_W), 1) // HEAD_DIM
    out = jnp.zeros((ROW_TILE, GROUP_W), F32)
    for h in range(N_HEADS):
        qh = jnp.where(lane_head == h, q, jnp.zeros_like(q))
        s = _nt_dot(qh, km_ref[...])
        p = jnp.exp(s - jnp.max(s, axis=1, keepdims=True))
        den = jnp.sum(p, axis=1, keepdims=True)
        pv = jnp.dot(p.astype(vm_ref.dtype), vm_ref[...], preferred_element_type=F32)
        out = jnp.where(lane_head == h, pv * (1.0 / den), out)
    o_ref[...] = (out * gate_ref[...]).astype(o_ref.dtype)


def _memory_attention(mq, km, vm, gates, bsz, seq):
    steps = seq // ROW_TILE
    n_mem = km.shape[1]
    return pl.pallas_call(
        _memattn_kernel,
        grid=(bsz, steps),
        in_specs=[pl.BlockSpec((ROW_TILE, GROUP_W), lambda b, i: (b * steps + i, 0)),
                  pl.BlockSpec((None, n_mem, GROUP_W), lambda b, i: (b, 0, 0)),
                  pl.BlockSpec((None, n_mem, GROUP_W), lambda b, i: (b, 0, 0)),
                  pl.BlockSpec((ROW_TILE, GROUP_W), lambda b, i: (b * steps + i, 3))],
        out_specs=pl.BlockSpec((ROW_TILE, GROUP_W), lambda b, i: (b * steps + i, 0)),
        out_shape=jax.ShapeDtypeStruct((bsz * seq, GROUP_W), mq.dtype),
        compiler_params=_cparams(("parallel", "arbitrary")),
        name="memory_attention",
    )(mq, km, vm, gates)


def _dsa_kernel(cqk_ref, cvt_ref, ik_ref, iq_ref, iw_ref, gate_ref, t5_ref, tri_ref, o_ref,
                key_ref, planes_ref, live_ref, am_ref, iwt_ref, qh_ref, lg_ref, lgb_ref, sa_ref, sb_ref,
                pa_ref, pb_ref, m_ref, mx_ref, alpha_ref, acc_ref):
    n = pl.program_id(1)
    nt = key_ref.shape[0]
    q0 = n * TILE
    n_pairs = n // 2 + 1
    score_scale = (IDX_DIM ** -0.5) * (IDX_HEADS ** -0.5)
    chunks = [slice(c * ROW_CHUNK, (c + 1) * ROW_CHUNK) for c in range(TILE // ROW_CHUNK)]

    q_limit = ((q0 + lax.broadcasted_iota(I32, (1, TILE), 1)) // CHUNK + 1) * CHUNK
    chunk_pos = lax.broadcasted_iota(I32, (ROW_CHUNK, 1), 0)

    def admissible(rows):
        return q0 + rows.start + chunk_pos < q_limit

    iwt_ref[...] = jnp.transpose(iw_ref[...])[0:IDX_HEADS, :] * score_scale

    def index_logits(j, buf):
        for p in range(IDX_HEADS // 2):
            for parity in range(2):
                buf[2 * p + parity] = _nt_dot(ik_ref[j, :, parity * 128:(parity + 1) * 128],
                                              iq_ref[:, p * 128:(p + 1) * 128])

    def index_keys(j, buf, diagonal):
        for rows in chunks:
            sc = None
            for h in range(IDX_HEADS):
                term = jnp.maximum(buf[h, rows, :], 0.0) * iwt_ref[h:h + 1, :]
                sc = term if sc is None else sc + term
            if diagonal:
                sc = jnp.where(admissible(rows), sc, NEG_INF)
            bits = pltpu.bitcast(sc, I32)
            sign = lax.shift_right_arithmetic(bits, 31)
            key_ref[j, rows, :] = (bits ^ (sign & 0x7FFFFFFF)) - sign
        for lanes in (slice(0, 128), slice(128, 256)):
            x = [key_ref[j, SUBLANES * i:SUBLANES * (i + 1), lanes] for i in range(32)]
            step, keep = 16, 0x0000FFFF
            while step:
                k = 0
                while k < 32:
                    t = (x[k] ^ lax.shift_right_logical(x[k + step], step)) & keep
                    x[k] = x[k] ^ t
                    x[k + step] = x[k + step] ^ lax.shift_left(t, step)
                    k = (k + step + 1) & ~step
                step >>= 1
                keep = (keep ^ (keep << step)) & 0xFFFFFFFF
            dst = pl.ds(pl.multiple_of(j * SUBLANES, SUBLANES), SUBLANES)
            for b in range(32):
                plane = x[31 - b]
                planes_ref[b, dst, lanes] = ~plane if b == 31 else plane

    index_logits(0, lg_ref)

    def far_score_pair(i, carry):
        j = 2 * i
        index_logits(j + 1, lgb_ref)
        index_keys(j, lg_ref, False)
        index_logits(j + 2, lg_ref)
        index_keys(j + 1, lgb_ref, False)
        return carry

    lax.fori_loop(0, n // 2, far_score_pair, 0)

    @pl.when(n % 2 == 1)
    def _():
        index_logits(n, lgb_ref)
        index_keys(n - 1, lg_ref, False)
        index_keys(n, lgb_ref, True)

    @pl.when(n % 2 == 0)
    def _():
        index_keys(n, lg_ref, True)

    group_tiles = math.gcd(GROUP_TILES, nt)
    n_groups = n // group_tiles + 1
    group_rows = group_tiles * SUBLANES

    def clear_tail(t, carry):
        dst = pl.ds(pl.multiple_of(t * SUBLANES, SUBLANES), SUBLANES)
        planes_ref[:, dst, :] = jnp.zeros((32, SUBLANES, TILE), I32)
        live_ref[dst, :] = jnp.zeros((SUBLANES, TILE), I32)
        return carry

    lax.fori_loop(n + 1, n_groups * group_tiles, clear_tail, 0)

    def mark_live(t, carry):
        live_ref[pl.ds(pl.multiple_of(t * SUBLANES, SUBLANES), SUBLANES), :] = jnp.full((SUBLANES, TILE), -1, I32)
        return carry

    lax.fori_loop(0, n + 1, mark_live, 0)

    def group(g):
        return pl.ds(pl.multiple_of(g * group_rows, group_rows), group_rows)

    def count_live(b):
        def body(g, c):
            ones = live_ref[group(g), :]
            if b is not None:
                ones = ones & planes_ref[b, group(g), :]
            pc = lax.population_count(ones)
            for r in range(group_tiles):
                c = c + pc[r * SUBLANES:(r + 1) * SUBLANES, :]
            return c
        c = lax.fori_loop(0, n_groups, body, jnp.zeros((SUBLANES, TILE), I32))
        return jnp.sum(c, axis=0, keepdims=True)

    def bit_step(t, carry):
        need, thr_u = carry
        b = 31 - t
        total = count_live(b)
        take = jnp.where(total >= need, -1, 0)
        need = jnp.where(total >= need, need, need - total)
        thr_u = thr_u | (take & lax.shift_left(jnp.int32(1), b))

        def narrow(g, c):
            live = live_ref[group(g), :]
            ones = live & planes_ref[b, group(g), :]
            live_ref[group(g), :] = (live ^ ones) ^ (live & take)
            return c

        lax.fori_loop(0, n_groups, narrow, 0)
        return need, thr_u

    quota, thr_u = lax.fori_loop(0, 32, bit_step,
                                 (jnp.full((1, TILE), TOPK, I32), jnp.zeros((1, TILE), I32)))
    thr = thr_u ^ SIGN_BIT
    n_equal = count_live(None)
    ties_cut = jnp.max(n_equal - quota) > 0

    am_ref[nt + 2 * N_HEADS] = jnp.full((TILE, TILE), NEG_INF, F32)

    def write_masks(with_ties):
        quota_f = quota.astype(F32)

        def selected(j, rows, taken, diagonal):
            k = key_ref[j, rows, :]
            open_ = jnp.where(admissible(rows), 0.0, NEG_INF) if diagonal else 0.0
            if not with_ties:
                return jnp.where(k >= thr, open_, NEG_INF), taken
            eq = jnp.where(k == thr, 1.0, 0.0).astype(BF16)
            before = jnp.dot(tri_ref[rows.start:rows.stop, rows.start:rows.stop], eq,
                             preferred_element_type=F32) + taken
            tied = jnp.where(k == thr, jnp.where(before < quota_f, open_, NEG_INF), NEG_INF)
            return jnp.where(k > thr, open_, tied), taken + jnp.sum(eq.astype(F32), axis=0, keepdims=True)

        def far_mask(j, taken):
            for rows in chunks:
                am, taken = selected(j, rows, taken, False)
                am_ref[j, rows, :] = am
            return taken

        taken = lax.fori_loop(0, n - 1, far_mask, jnp.zeros((1, TILE), F32))

        def near_mask(j, d, taken):
            for rows in chunks:
                am, taken = selected(j, rows, taken, d == 1)
                for h in range(N_HEADS):
                    am_ref[nt + N_HEADS * d + h, rows, :] = am + t5_ref[h, d * TILE + rows.start:d * TILE + rows.stop, :]
            return taken

        prev = near_mask(jnp.maximum(n - 1, 0), 0, taken)
        near_mask(n, 1, jnp.where(n >= 1, prev, taken))

    @pl.when(ties_cut)
    def _():
        write_masks(True)

    @pl.when(jnp.logical_not(ties_cut))
    def _():
        write_masks(False)

    m_ref[...] = jnp.full(m_ref.shape, -1e30, F32)
    acc_ref[...] = jnp.zeros(acc_ref.shape, F32)
    alpha_ref[...] = jnp.ones(alpha_ref.shape, F32)
    pb_ref[...] = jnp.zeros(pb_ref.shape, pb_ref.dtype)
    q = cqk_ref[n, :, 0:GROUP_W]
    lane_head = lax.broadcasted_iota(I32, (TILE, GROUP_W), 1) // HEAD_DIM
    for h in range(N_HEADS):
        qh_ref[h] = jnp.where(lane_head == h, q, jnp.zeros_like(q))

    def masked_logits(j, s_buf):
        k = cqk_ref[jnp.minimum(j, n), :, GROUP_W:2 * GROUP_W]
        near = j - (n - 1)
        for h in range(N_HEADS):
            slab = jnp.where(j > n, nt + 2 * N_HEADS, jnp.where(near >= 0, nt + N_HEADS * near + h, j))
            s = _nt_dot(k, qh_ref[h]) + am_ref[slab]
            s_buf[h] = s
            mx_ref[j % 2, h] = jnp.max(s, axis=0, keepdims=True)

    def softmax_step(j, s_buf, p_buf):
        for h in range(N_HEADS):
            m_old = m_ref[h]
            m_new = jnp.maximum(m_old, mx_ref[j % 2, h])
            for rows in chunks:
                p_buf[h, rows, :] = jnp.exp2(s_buf[h, rows, :] - m_new).astype(p_buf.dtype)
            alpha_ref[h] = jnp.exp2(m_old - m_new)
            m_ref[h] = m_new

    def accumulate(j, p_buf):
        jc = jnp.clip(j, 0, n)
        for h in range(N_HEADS):
            vt = cvt_ref[jc, h * V_ROWS:(h + 1) * V_ROWS, :]
            acc_ref[h] = alpha_ref[h] * acc_ref[h] + jnp.dot(vt, p_buf[h], preferred_element_type=F32)

    masked_logits(0, sa_ref)

    def tile_pair(i, carry):
        j = 2 * i
        accumulate(j - 1, pb_ref)
        masked_logits(j + 1, sb_ref)
        softmax_step(j, sa_ref, pa_ref)
        masked_logits(j + 2, sa_ref)
        accumulate(j, pa_ref)
        softmax_step(j + 1, sb_ref, pb_ref)
        return carry

    lax.fori_loop(0, n_pairs, tile_pair, 0)
    accumulate(2 * n_pairs - 1, pb_ref)

    out_t = jnp.concatenate([acc_ref[h, 0:HEAD_DIM, :] * (1.0 / acc_ref[h, HEAD_DIM:HEAD_DIM + 1, :])
                             for h in range(N_HEADS)], axis=0)
    o_ref[...] = (jnp.transpose(out_t) * gate_ref[...]).astype(o_ref.dtype)


def _dsa_attention(cqk4, cvt, ik4, iq, iw, gates, t5_rel, tri, bsz, seq):
    nt = seq // TILE
    assert nt % 2 == 0
    sdt = cqk4.dtype
    return pl.pallas_call(
        _dsa_kernel,
        grid=(bsz, nt),
        in_specs=[_resident((None, nt, TILE, 512), lambda b, i: (b, 0, 0, 0)),
                  _resident((None, nt, N_HEADS * V_ROWS, TILE), lambda b, i: (b, 0, 0, 0)),
                  _resident((None, nt, TILE, 256), lambda b, i: (b, 0, 0, 0)),
                  pl.BlockSpec((TILE, 512), lambda b, i: (b * nt + i, 0)),
                  pl.BlockSpec((TILE, 128), lambda b, i: (b * nt + i, 0)),
                  pl.BlockSpec((TILE, GROUP_W), lambda b, i: (b * nt + i, 2)),
                  _resident((N_HEADS, 2 * TILE, TILE), lambda b, i: (0, 0, 0)),
                  _resident((TILE, TILE), lambda b, i: (0, 0))],
        out_specs=pl.BlockSpec((TILE, GROUP_W), lambda b, i: (b * nt + i, 0)),
        out_shape=jax.ShapeDtypeStruct((bsz * seq, GROUP_W), sdt),
        scratch_shapes=[pltpu.VMEM((nt, TILE, TILE), I32),
                        pltpu.VMEM((32, nt * SUBLANES, TILE), I32),
                        pltpu.VMEM((nt * SUBLANES, TILE), I32),
                        pltpu.VMEM((nt + 2 * N_HEADS + 1, TILE, TILE), F32),
                        pltpu.VMEM((IDX_HEADS, TILE), F32),
                        pltpu.VMEM((N_HEADS, TILE, GROUP_W), sdt),
                        pltpu.VMEM((IDX_HEADS, TILE, TILE), F32),
                        pltpu.VMEM((IDX_HEADS, TILE, TILE), F32),
                        pltpu.VMEM((N_HEADS, TILE, TILE), F32),
                        pltpu.VMEM((N_HEADS, TILE, TILE), F32),
                        pltpu.VMEM((N_HEADS, TILE, TILE), sdt),
                        pltpu.VMEM((N_HEADS, TILE, TILE), sdt),
                        pltpu.VMEM((N_HEADS, 1, TILE), F32),
                        pltpu.VMEM((2, N_HEADS, 1, TILE), F32),
                        pltpu.VMEM((N_HEADS, 1, TILE), F32),
                        pltpu.VMEM((N_HEADS, V_ROWS, TILE), F32)],
        compiler_params=_cparams(("parallel", "arbitrary")),
        name="dsa_attention",
    )(cqk4, cvt, ik4, iq, iw, gates, t5_rel, tri)


def _outproj_kernel(h_ref, ya_ref, yb_ref, yc_ref, ym_ref, w_ref, b_ref, g_ref, beta_ref, o_ref):
    y = b_ref[...]
    for idx, part in enumerate((ya_ref, yb_ref, yc_ref, ym_ref)):
        y = y + jnp.dot(part[...], w_ref[idx * GROUP_W:(idx + 1) * GROUP_W, :],
                        preferred_element_type=F32)
    o_ref[...] = _layer_norm(DEEPNORM_ALPHA * h_ref[...] + y, g_ref[...], beta_ref[...])


def _out_projection(h, ya, yb, yc, ym, w_out, b_out, g, beta):
    m = h.shape[0]
    part = pl.BlockSpec((ROW_TILE, GROUP_W), lambda i: (i, 0))
    vec = pl.BlockSpec((1, D_MODEL), lambda i: (0, 0))
    return pl.pallas_call(
        _outproj_kernel,
        grid=(m // ROW_TILE,),
        in_specs=[pl.BlockSpec((ROW_TILE, D_MODEL), lambda i: (i, 0)), part, part, part, part,
                  pl.BlockSpec((D_MODEL, D_MODEL), lambda i: (0, 0)), vec, vec, vec],
        out_specs=pl.BlockSpec((ROW_TILE, D_MODEL), lambda i: (i, 0)),
        out_shape=jax.ShapeDtypeStruct((m, D_MODEL), F32),
        compiler_params=_cparams(("parallel",)),
        name="out_projection",
    )(h, ya, yb, yc, ym, w_out, b_out, g, beta)


def _prep_in_weights(w_in, b_in, sdt):
    sp = np.cumsum((0, 256, 256, 256, 256, 256, 256, 256, 256, 256, 256, 256, 512, 64, 8, 256, 256))
    names = ("a_u", "a_v", "a_g", "bq", "bk", "bv", "bg", "cq", "ck", "cv", "cg", "iq", "ik", "iw", "mq", "mg")
    col = {nm: slice(int(sp[i]), int(sp[i + 1])) for i, nm in enumerate(names)}
    qs = HEAD_DIM ** -0.5

    def build(t):
        z = lambda width: jnp.zeros((t.shape[0], width), t.dtype)
        ik = t[:, col["ik"]]
        main = [t[:, col["a_u"]], t[:, col["a_v"]],
                t[:, col["bq"]] * qs, t[:, col["bk"]], t[:, col["bv"]],
                t[:, col["cq"]] * (qs * LOG2E), t[:, col["ck"]],
                t[:, col["mq"]] * qs,
                t[:, col["a_g"]], t[:, col["bg"]], t[:, col["cg"]], t[:, col["mg"]]]
        idx = [t[:, col["iq"]], ik, z(64), z(64), ik, t[:, col["iw"]], z(120)]
        return jnp.concatenate(main, axis=1), jnp.concatenate(idx, axis=1), t[:, col["cv"]]

    w_main, w_idx, w_cv = build(w_in)
    b_main, b_idx, b_cv = build(b_in.reshape(1, -1))
    return (w_main.astype(sdt), b_main, w_idx, b_idx, jnp.transpose(w_cv).astype(sdt), jnp.transpose(b_cv))


def _toeplitz(g, n_rows, n_cols):
    span = n_rows + n_cols - 1
    lead = g.shape[:-1]
    period = jnp.concatenate([g, jnp.zeros(lead + (1,), g.dtype)], axis=-1)
    flat = jnp.tile(period, (1,) * len(lead) + (n_rows,))[..., :n_rows * span]
    return flat.reshape(lead + (n_rows, span))[..., n_rows - 1:n_rows - 1 + n_cols]


def _band_bias_table(b_rel):
    m = np.arange(4 * TILE - 1)
    rel_idx = np.clip(3 * TILE - 1 - m, -REL_CLIP, REL_CLIP) + REL_CLIP
    bias = _toeplitz(b_rel[:, jnp.asarray(rel_idx)], TILE, 3 * TILE)
    i = np.arange(TILE)[:, None]
    j = np.arange(3 * TILE)[None, :]
    chunk_diff = i // CHUNK - j // CHUNK + B_PREV_CHUNKS
    allowed = (chunk_diff >= 0) & (chunk_diff <= B_PREV_CHUNKS)
    return jnp.where(jnp.asarray(allowed)[None], bias, NEG_INF).astype(F32)


def _t5_bucket(rel):
    nb = T5_BUCKETS // 2
    max_exact = nb // 2
    ret = jnp.where(rel > 0, nb, 0)
    n = jnp.abs(rel)
    nf = jnp.maximum(n, 1).astype(F32)
    large = max_exact + (jnp.log(nf / max_exact) / math.log(T5_MAX_DIST / max_exact)
                         * (nb - max_exact)).astype(I32)
    large = jnp.minimum(large, nb - 1)
    return ret + jnp.where(n < max_exact, n, large)


def _t5_rel_table(t5_table):
    m = jnp.arange(3 * TILE - 1, dtype=I32)
    near = t5_table[_t5_bucket(TILE - 1 - m)]
    far = t5_table[_t5_bucket(jnp.full((1,), -(2 * TILE + 1), I32))]
    return _toeplitz(jnp.transpose(near - far) * LOG2E, 2 * TILE, TILE).astype(F32)


def kernel(x, mem, ln_in_g, ln_in_b, w_in, b_in, a_ln_g, a_ln_b, a_ws, a_bs, b_rel, t5_table,
           w_mem_kv, w_out, b_out, ln_g, ln_b):
    bsz, seq, _ = x.shape
    assert seq % ROW_TILE == 0 and x.shape[2] == D_MODEL
    m = bsz * seq
    nt = seq // TILE

    cpos = np.arange(A_BLOCK) // CHUNK
    a_mask = jnp.asarray(cpos[None, :] <= cpos[:, None])
    t5_rel = _t5_rel_table(t5_table)
    tri = jnp.asarray(np.tril(np.ones((TILE, TILE), np.float32), -1)).astype(BF16)

    h = _input_norm(x.reshape(m, D_MODEL), ln_in_g, ln_in_b)
    for l in range(DEPTH):
        sdt = BF16 if l == DEPTH - 1 else F32
        w_main, b_main, w_idx, b_idx, wvt, bvt = _prep_in_weights(w_in[l], b_in[l], sdt)
        uv, bqkv, cqk, cvt, iq, ik, iw, mq, gates = _in_projection(h, w_main, b_main, w_idx, b_idx, wvt, bvt,
                                                                   bsz, seq)

        ws_masked = jnp.where(a_mask[None], a_ws[l], 0.0).astype(sdt)
        bs_lanes = jnp.repeat(jnp.transpose(a_bs[l]), HEAD_DIM, axis=1)
        ya = _gmlp(uv, gates, a_ln_g[l].reshape(1, -1), a_ln_b[l].reshape(1, -1), ws_masked, bs_lanes)

        yb = _band_attention(bqkv.reshape(bsz, nt, TILE, 768), gates, _band_bias_table(b_rel[l]), bsz, seq)

        yc = _dsa_attention(cqk.reshape(bsz, nt, TILE, 512), cvt, ik.reshape(bsz, nt, TILE, 256),
                            iq, iw, gates, t5_rel, tri, bsz, seq)

        km, vm = _memory_kv(mem, w_mem_kv[l].astype(sdt))
        ym = _memory_attention(mq, km, vm, gates, bsz, seq)

        h = _out_projection(h, ya, yb, yc, ym, w_out[l].astype(sdt), b_out[l].reshape(1, -1),
                            ln_g[l].reshape(1, -1), ln_b[l].reshape(1, -1))
    return h.reshape(bsz, seq, D_MODEL)
```

```python
import functools
import math

import numpy as np
import jax
import jax.numpy as jnp
from jax import lax
from jax.experimental import pallas as pl
from jax.experimental.pallas import tpu as pltpu

F32 = jnp.float32
BF16 = jnp.bfloat16
I32 = jnp.int32

D_MODEL = 1024
CHUNK = 64
HEAD_DIM = 64
GROUP_W = 256
N_HEADS = GROUP_W // HEAD_DIM
A_BLOCK = 128
B_PREV_CHUNKS = 8
REL_CLIP = 128
IDX_HEADS = 8
IDX_DIM = 64
TOPK = 256
T5_BUCKETS = 32
T5_MAX_DIST = 128
N_MEM = 256
DEPTH = 2
DEEPNORM_ALPHA = (2 * DEPTH) ** 0.25
LN_EPS = 1e-5

TILE = 256
ROW_TILE = 512
VMEM_LIMIT = 56 * 1024 * 1024
NEG_INF = float("-inf")
SIGN_BIT = -2 ** 31
SUBLANES = 8
GROUP_TILES = 8
V_ROWS = HEAD_DIM + 16
ROW_CHUNK = 64
LOG2E = 1.4426950408889634

C_UV = (0, 512)
C_BQKV = (512, 1280)
C_CQK = (1280, 1792)
C_MQ = (1792, 2048)
C_GATE = (2048, 3072)
N_MAIN = 3072
C_IQ = (0, 512)
C_IK = (512, 768)
C_IW = (768, 896)
N_IDX = 896


def _cparams(sem):
    return pltpu.CompilerParams(dimension_semantics=sem, vmem_limit_bytes=VMEM_LIMIT)


def _resident(block_shape, index_map):
    return pl.BlockSpec(block_shape, index_map, pipeline_mode=pl.Buffered(1))


def _nt_dot(a, b):
    return lax.dot_general(a, b, (((1,), (1,)), ((), ())), preferred_element_type=F32)


def _gelu_tanh(x):
    return 0.5 * x * (1.0 + jnp.tanh(math.sqrt(2.0 / math.pi) * (x + 0.044715 * (x * x * x))))


def _silu(x):
    return x * (1.0 / (1.0 + jnp.exp(-x)))


def _layer_norm(x, g, b):
    mu = jnp.mean(x, axis=-1, keepdims=True)
    xc = x - mu
    var = jnp.mean(xc * xc, axis=-1, keepdims=True)
    return xc * lax.rsqrt(var + LN_EPS) * g + b


def _inproj_kernel(*refs, normalise_input):
    if normalise_input:
        h_ref, lng_ref, lnb_ref, *refs = refs
        *refs, hout_ref = refs
    else:
        h_ref, *refs = refs
    (w_ref, b_ref, widx_ref, bidx_ref, wvt_ref, bvt_ref,
     uv_ref, bqkv_ref, cqk_ref, cvt_ref, iq_ref, ik_ref, iw_ref, mq_ref, gate_ref) = refs
    x = h_ref[...]
    if normalise_input:
        x = _layer_norm(x, lng_ref[...], lnb_ref[...])
        hout_ref[...] = x
    xm = x.astype(w_ref.dtype)

    def seg(c):
        return jnp.dot(xm, w_ref[:, c[0]:c[1]], preferred_element_type=F32) + b_ref[:, c[0]:c[1]]

    def idx_seg(c):
        return jnp.dot(x, widx_ref[:, c[0]:c[1]], preferred_element_type=F32) + bidx_ref[:, c[0]:c[1]]

    uv_ref[...] = _gelu_tanh(seg(C_UV))
    bqkv_ref[...] = seg(C_BQKV).astype(bqkv_ref.dtype)
    cqk_ref[...] = seg(C_CQK).astype(cqk_ref.dtype)
    mq_ref[...] = seg(C_MQ).astype(mq_ref.dtype)
    gate_ref[...] = _silu(seg(C_GATE))
    iq_ref[...] = idx_seg(C_IQ)
    ik_ref[...] = idx_seg(C_IK)
    iw_ref[...] = idx_seg(C_IW)
    for t in range(ROW_TILE // TILE):
        vt = (_nt_dot(wvt_ref[...], xm[t * TILE:(t + 1) * TILE, :]) + bvt_ref[...]).astype(cvt_ref.dtype)
        for h in range(N_HEADS):
            cvt_ref[t, h * V_ROWS:h * V_ROWS + HEAD_DIM, :] = vt[h * HEAD_DIM:(h + 1) * HEAD_DIM, :]
            cvt_ref[t, h * V_ROWS + HEAD_DIM:(h + 1) * V_ROWS, :] = jnp.ones((V_ROWS - HEAD_DIM, TILE), cvt_ref.dtype)


def _in_projection(h, w_main, b_main, w_idx, b_idx, wvt, bvt, bsz, seq, input_norm=None):
    m = h.shape[0]
    sdt = w_main.dtype
    steps_per_batch = seq // ROW_TILE
    tiles_per_step = ROW_TILE // TILE

    def rows(width, dtype):
        return (pl.BlockSpec((ROW_TILE, width), lambda i: (i, 0)),
                jax.ShapeDtypeStruct((m, width), dtype))

    outs = [rows(512, F32), rows(768, sdt), rows(512, sdt),
            (pl.BlockSpec((None, tiles_per_step, N_HEADS * V_ROWS, TILE),
                          lambda i: (i // steps_per_batch, i % steps_per_batch, 0, 0)),
             jax.ShapeDtypeStruct((bsz, seq // TILE, N_HEADS * V_ROWS, TILE), sdt)),
            rows(512, F32), rows(256, F32), rows(128, F32), rows(256, sdt), rows(1024, F32)]
    vec = pl.BlockSpec((1, D_MODEL), lambda i: (0, 0))
    norm_specs, norm_args = [], []
    if input_norm is not None:
        outs.append(rows(D_MODEL, F32))
        norm_specs = [vec, vec]
        norm_args = [p.reshape(1, D_MODEL) for p in input_norm]
    return pl.pallas_call(
        functools.partial(_inproj_kernel, normalise_input=input_norm is not None),
        grid=(m // ROW_TILE,),
        in_specs=[pl.BlockSpec((ROW_TILE, D_MODEL), lambda i: (i, 0)), *norm_specs,
                  _resident((D_MODEL, N_MAIN), lambda i: (0, 0)),
                  pl.BlockSpec((1, N_MAIN), lambda i: (0, 0)),
                  _resident((D_MODEL, N_IDX), lambda i: (0, 0)),
                  pl.BlockSpec((1, N_IDX), lambda i: (0, 0)),
                  _resident((GROUP_W, D_MODEL), lambda i: (0, 0)),
                  pl.BlockSpec((GROUP_W, 1), lambda i: (0, 0))],
        out_specs=[o[0] for o in outs],
        out_shape=[o[1] for o in outs],
        compiler_params=_cparams(("parallel",)),
        name="in_projection",
    )(h, *norm_args, w_main, b_main, w_idx, b_idx, wvt, bvt)


def _gmlp_kernel(uv_ref, gate_ref, lng_ref, lnb_ref, ws_ref, bs_ref, o_ref):
    lane_group = lax.broadcasted_iota(I32, (A_BLOCK, GROUP_W), 1) // HEAD_DIM
    for blk in range(ROW_TILE // A_BLOCK):
        r = slice(blk * A_BLOCK, (blk + 1) * A_BLOCK)
        u = uv_ref[r, 0:GROUP_W]
        v = _layer_norm(uv_ref[r, GROUP_W:2 * GROUP_W], lng_ref[...], lnb_ref[...])
        vb = v.astype(ws_ref.dtype)
        mixed = bs_ref[...]
        for g in range(N_HEADS):
            mg = jnp.dot(ws_ref[g], vb, preferred_element_type=F32)
            mixed = mixed + jnp.where(lane_group == g, mg, 0.0)
        o_ref[r, :] = ((u * mixed) * gate_ref[r, :]).astype(o_ref.dtype)


def _gmlp(uv, gates, lng, lnb, ws_masked, bs_lanes):
    m = uv.shape[0]
    return pl.pallas_call(
        _gmlp_kernel,
        grid=(m // ROW_TILE,),
        in_specs=[pl.BlockSpec((ROW_TILE, 512), lambda i: (i, 0)),
                  pl.BlockSpec((ROW_TILE, GROUP_W), lambda i: (i, 0)),
                  pl.BlockSpec((1, GROUP_W), lambda i: (0, 0)),
                  pl.BlockSpec((1, GROUP_W), lambda i: (0, 0)),
                  pl.BlockSpec((N_HEADS, A_BLOCK, A_BLOCK), lambda i: (0, 0, 0)),
                  pl.BlockSpec((A_BLOCK, GROUP_W), lambda i: (0, 0))],
        out_specs=pl.BlockSpec((ROW_TILE, GROUP_W), lambda i: (i, 0)),
        out_shape=jax.ShapeDtypeStruct((m, GROUP_W), ws_masked.dtype),
        compiler_params=_cparams(("parallel",)),
        name="gmlp",
    )(uv, gates, lng, lnb, ws_masked, bs_lanes)


def _band_kernel(qkv_ref, gate_ref, bias_ref, o_ref):
    i = pl.program_id(1)
    q = qkv_ref[i, :, 0:GROUP_W]
    lane_head = lax.broadcasted_iota(I32, (TILE, GROUP_W), 1) // HEAD_DIM
    ks, vs, live = [], [], []
    for d in (2, 1, 0):
        kb = i - d
        kbc = jnp.maximum(kb, 0)
        ks.append(qkv_ref[kbc, :, GROUP_W:2 * GROUP_W])
        vs.append(qkv_ref[kbc, :, 2 * GROUP_W:3 * GROUP_W])
        live.append(kb >= 0)
    out = jnp.zeros((TILE, GROUP_W), F32)
    for h in range(N_HEADS):
        qh = jnp.where(lane_head == h, q, jnp.zeros_like(q))
        ss = []
        for t in range(3):
            s = _nt_dot(qh, ks[t]) + bias_ref[h, :, t * TILE:(t + 1) * TILE]
            ss.append(jnp.where(live[t], s, NEG_INF))
        mx = jnp.maximum(jnp.maximum(jnp.max(ss[0], axis=1, keepdims=True),
                                     jnp.max(ss[1], axis=1, keepdims=True)),
                         jnp.max(ss[2], axis=1, keepdims=True))
        den = jnp.zeros((TILE, 1), F32)
        pv = jnp.zeros((TILE, GROUP_W), F32)
        for t in range(3):
            p = jnp.exp(ss[t] - mx)
            den = den + jnp.sum(p, axis=1, keepdims=True)
            pv = pv + jnp.dot(p.astype(vs[t].dtype), vs[t], preferred_element_type=F32)
        out = jnp.where(lane_head == h, pv * (1.0 / den), out)
    o_ref[...] = (out * gate_ref[...]).astype(o_ref.dtype)


def _band_attention(bqkv4, gates, bias, bsz, seq):
    nt = seq // TILE
    return pl.pallas_call(
        _band_kernel,
        grid=(bsz, nt),
        in_specs=[pl.BlockSpec((None, nt, TILE, 768), lambda b, i: (b, 0, 0, 0)),
                  pl.BlockSpec((TILE, GROUP_W), lambda b, i: (b * nt + i, 1)),
                  pl.BlockSpec((N_HEADS, TILE, 3 * TILE), lambda b, i: (0, 0, 0))],
        out_specs=pl.BlockSpec((TILE, GROUP_W), lambda b, i: (b * nt + i, 0)),
        out_shape=jax.ShapeDtypeStruct((bsz * seq, GROUP_W), bqkv4.dtype),
        compiler_params=_cparams(("parallel", "arbitrary")),
        name="band_attention",
    )(bqkv4, gates, bias)


def _memkv_kernel(mem_ref, w_ref, km_ref, vm_ref):
    kv = jnp.dot(mem_ref[...].astype(w_ref.dtype), w_ref[...], preferred_element_type=F32)
    km_ref[...] = kv[:, 0:GROUP_W].astype(km_ref.dtype)
    vm_ref[...] = kv[:, GROUP_W:2 * GROUP_W].astype(vm_ref.dtype)


def _memory_kv(mem, w_kv):
    bsz, n_mem, _ = mem.shape
    blk = pl.BlockSpec((None, n_mem, GROUP_W), lambda b: (b, 0, 0))
    shp = jax.ShapeDtypeStruct((bsz, n_mem, GROUP_W), w_kv.dtype)
    return pl.pallas_call(
        _memkv_kernel,
        grid=(bsz,),
        in_specs=[pl.BlockSpec((None, n_mem, D_MODEL), lambda b: (b, 0, 0)),
                  pl.BlockSpec((D_MODEL, 2 * GROUP_W), lambda b: (0, 0))],
        out_specs=[blk, blk],
        out_shape=[shp, shp],
        compiler_params=_cparams(("parallel",)),
        name="memory_kv",
    )(mem, w_kv)


def _memattn_kernel(q_ref, km_ref, vm_ref, gate_ref, o_ref):
    q = q_ref[...]
    lane_head = lax.broadcasted_iota(I32, (ROW_TILE, GROUP_W), 1) // HEAD_DIM
    out = jnp.zeros((ROW_TILE, GROUP_W), F32)
    for h in range(N_HEADS):
        qh = jnp.where(lane_head == h, q, jnp.zeros_like(q))
        s = _nt_dot(qh, km_ref[...])
        p = jnp.exp(s - jnp.max(s, axis=1, keepdims=True))
        den = jnp.sum(p, axis=1, keepdims=True)
        pv = jnp.dot(p.astype(vm_ref.dtype), vm_ref[...], preferred_element_type=F32)
        out = jnp.where(lane_head == h, pv * (1.0 / den), out)
    o_ref[...] = (out * gate_ref[...]).astype(o_ref.dtype)


def _memory_attention(mq, km, vm, gates, bsz, seq):
    steps = seq // ROW_TILE
    n_mem = km.shape[1]
    return pl.pallas_call(
        _memattn_kernel,
        grid=(bsz, steps),
        in_specs=[pl.BlockSpec((ROW_TILE, GROUP_W), lambda b, i: (b * steps + i, 0)),
                  pl.BlockSpec((None, n_mem, GROUP_W), lambda b, i: (b, 0, 0)),
                  pl.BlockSpec((None, n_mem, GROUP_W), lambda b, i: (b, 0, 0)),
                  pl.BlockSpec((ROW_TILE, GROUP_W), lambda b, i: (b * steps + i, 3))],
        out_specs=pl.BlockSpec((ROW_TILE, GROUP_W), lambda b, i: (b * steps + i, 0)),
        out_shape=jax.ShapeDtypeStruct((bsz * seq, GROUP_W), mq.dtype),
        compiler_params=_cparams(("parallel", "arbitrary")),
        name="memory_attention",
    )(mq, km, vm, gates)


def _dsa_kernel(cqk_ref, cvt_ref, ik_ref, iq_ref, iw_ref, gate_ref, t5_ref, tri_ref, o_ref,
                key_ref, planes_ref, live_ref, am_ref, iwt_ref, qh_ref, lg_ref, lgb_ref, sa_ref, sb_ref,
                pa_ref, pb_ref, m_ref, mx_ref, alpha_ref, acc_ref):
    n = pl.program_id(1)
    nt = key_ref.shape[0]
    q0 = n * TILE
    n_pairs = n // 2 + 1
    score_scale = (IDX_DIM ** -0.5) * (IDX_HEADS ** -0.5)
    chunks = [slice(c * ROW_CHUNK, (c + 1) * ROW_CHUNK) for c in range(TILE // ROW_CHUNK)]

    q_limit = ((q0 + lax.broadcasted_iota(I32, (1, TILE), 1)) // CHUNK + 1) * CHUNK
    chunk_pos = lax.broadcasted_iota(I32, (ROW_CHUNK, 1), 0)

    def admissible(rows):
        return q0 + rows.start + chunk_pos < q_limit

    iwt_ref[...] = jnp.transpose(iw_ref[...])[0:IDX_HEADS, :] * score_scale

    def index_logits(j, buf):
        for p in range(IDX_HEADS // 2):
            for parity in range(2):
                buf[2 * p + parity] = _nt_dot(ik_ref[j, :, parity * 128:(parity + 1) * 128],
                                              iq_ref[:, p * 128:(p + 1) * 128])

    def index_keys(j, buf, diagonal):
        for rows in chunks:
            sc = None
            for h in range(IDX_HEADS):
                term = jnp.maximum(buf[h, rows, :], 0.0) * iwt_ref[h:h + 1, :]
                sc = term if sc is None else sc + term
            if diagonal:
                sc = jnp.where(admissible(rows), sc, NEG_INF)
            bits = pltpu.bitcast(sc, I32)
            sign = lax.shift_right_arithmetic(bits, 31)
            key_ref[j, rows, :] = (bits ^ (sign & 0x7FFFFFFF)) - sign
        for lanes in (slice(0, 128), slice(128, 256)):
            x = [key_ref[j, SUBLANES * i:SUBLANES * (i + 1), lanes] for i in range(32)]
            step, keep = 16, 0x0000FFFF
            while step:
                k = 0
                while k < 32:
                    t = (x[k] ^ lax.shift_right_logical(x[k + step], step)) & keep
                    x[k] = x[k] ^ t
                    x[k + step] = x[k + step] ^ lax.shift_left(t, step)
                    k = (k + step + 1) & ~step
                step >>= 1
                keep = (keep ^ (keep << step)) & 0xFFFFFFFF
            dst = pl.ds(pl.multiple_of(j * SUBLANES, SUBLANES), SUBLANES)
            for b in range(32):
                plane = x[31 - b]
                planes_ref[b, dst, lanes] = ~plane if b == 31 else plane

    index_logits(0, lg_ref)

    def far_score_pair(i, carry):
        j = 2 * i
        index_logits(j + 1, lgb_ref)
        index_keys(j, lg_ref, False)
        index_logits(j + 2, lg_ref)
        index_keys(j + 1, lgb_ref, False)
        return carry

    lax.fori_loop(0, n // 2, far_score_pair, 0)

    @pl.when(n % 2 == 1)
    def _():
        index_logits(n, lgb_ref)
        index_keys(n - 1, lg_ref, False)
        index_keys(n, lgb_ref, True)

    @pl.when(n % 2 == 0)
    def _():
        index_keys(n, lg_ref, True)

    group_tiles = math.gcd(GROUP_TILES, nt)
    n_groups = n // group_tiles + 1
    group_rows = group_tiles * SUBLANES

    def clear_tail(t, carry):
        dst = pl.ds(pl.multiple_of(t * SUBLANES, SUBLANES), SUBLANES)
        planes_ref[:, dst, :] = jnp.zeros((32, SUBLANES, TILE), I32)
        live_ref[dst, :] = jnp.zeros((SUBLANES, TILE), I32)
        return carry

    lax.fori_loop(n + 1, n_groups * group_tiles, clear_tail, 0)

    def mark_live(t, carry):
        live_ref[pl.ds(pl.multiple_of(t * SUBLANES, SUBLANES), SUBLANES), :] = jnp.full((SUBLANES, TILE), -1, I32)
        return carry

    lax.fori_loop(0, n + 1, mark_live, 0)

    def group(g):
        return pl.ds(pl.multiple_of(g * group_rows, group_rows), group_rows)

    def count_live(b):
        def body(g, c):
            ones = live_ref[group(g), :]
            if b is not None:
                ones = ones & planes_ref[b, group(g), :]
            pc = lax.population_count(ones)
            for r in range(group_tiles):
                c = c + pc[r * SUBLANES:(r + 1) * SUBLANES, :]
            return c
        c = lax.fori_loop(0, n_groups, body, jnp.zeros((SUBLANES, TILE), I32))
        return jnp.sum(c, axis=0, keepdims=True)

    def bit_step(t, carry):
        need, thr_u, total = carry
        b = 31 - t
        take = jnp.where(total >= need, -1, 0)
        need = jnp.where(total >= need, need, need - total)
        thr_u = thr_u | (take & lax.shift_left(jnp.int32(1), b))
        b_next = jnp.maximum(b - 1, 0)

        def narrow_and_count(g, c):
            live = live_ref[group(g), :]
            ones = live & planes_ref[b, group(g), :]
            live = (live ^ ones) ^ (live & take)
            live_ref[group(g), :] = live
            pc = lax.population_count(live & planes_ref[b_next, group(g), :])
            for r in range(group_tiles):
                c = c + pc[r * SUBLANES:(r + 1) * SUBLANES, :]
            return c

        c = lax.fori_loop(0, n_groups, narrow_and_count, jnp.zeros((SUBLANES, TILE), I32))
        return need, thr_u, jnp.sum(c, axis=0, keepdims=True)

    quota, thr_u, _ = lax.fori_loop(0, 32, bit_step,
                                    (jnp.full((1, TILE), TOPK, I32), jnp.zeros((1, TILE), I32), count_live(31)))
    thr = thr_u ^ SIGN_BIT
    n_equal = count_live(None)
    ties_cut = jnp.max(n_equal - quota) > 0

    am_ref[nt + 2 * N_HEADS] = jnp.full((TILE, TILE), NEG_INF, F32)

    def write_masks(with_ties):
        quota_f = quota.astype(F32)

        def selected(j, rows, taken, diagonal):
            k = key_ref[j, rows, :]
            open_ = jnp.where(admissible(rows), 0.0, NEG_INF) if diagonal else 0.0
            if not with_ties:
                return jnp.where(k >= thr, open_, NEG_INF), taken
            eq = jnp.where(k == thr, 1.0, 0.0).astype(BF16)
            before = jnp.dot(tri_ref[rows.start:rows.stop, rows.start:rows.stop], eq,
                             preferred_element_type=F32) + taken
            tied = jnp.where(k == thr, jnp.where(before < quota_f, open_, NEG_INF), NEG_INF)
            return jnp.where(k > thr, open_, tied), taken + jnp.sum(eq.astype(F32), axis=0, keepdims=True)

        def far_mask(j, taken):
            for rows in chunks:
                am, taken = selected(j, rows, taken, False)
                am_ref[j, rows, :] = am
            return taken

        taken = lax.fori_loop(0, n - 1, far_mask, jnp.zeros((1, TILE), F32))

        def near_mask(j, d, taken):
            for rows in chunks:
                am, taken = selected(j, rows, taken, d == 1)
                for h in range(N_HEADS):
                    am_ref[nt + N_HEADS * d + h, rows, :] = am + t5_ref[h, d * TILE + rows.start:d * TILE + rows.stop, :]
            return taken

        prev = near_mask(jnp.maximum(n - 1, 0), 0, taken)
        near_mask(n, 1, jnp.where(n >= 1, prev, taken))

    @pl.when(ties_cut)
    def _():
        write_masks(True)

    @pl.when(jnp.logical_not(ties_cut))
    def _():
        write_masks(False)

    m_ref[...] = jnp.full(m_ref.shape, -1e30, F32)
    acc_ref[...] = jnp.zeros(acc_ref.shape, F32)
    alpha_ref[...] = jnp.ones(alpha_ref.shape, F32)
    pb_ref[...] = jnp.zeros(pb_ref.shape, pb_ref.dtype)
    q = cqk_ref[n, :, 0:GROUP_W]
    lane_head = lax.broadcasted_iota(I32, (TILE, GROUP_W), 1) // HEAD_DIM
    for h in range(N_HEADS):
        qh_ref[h] = jnp.where(lane_head == h, q, jnp.zeros_like(q))

    def masked_logits(j, s_buf):
        k = cqk_ref[jnp.minimum(j, n), :, GROUP_W:2 * GROUP_W]
        near = j - (n - 1)
        for h in range(N_HEADS):
            slab = jnp.where(j > n, nt + 2 * N_HEADS, jnp.where(near >= 0, nt + N_HEADS * near + h, j))
            s = _nt_dot(k, qh_ref[h]) + am_ref[slab]
            s_buf[h] = s
            mx_ref[j % 2, h] = jnp.max(s, axis=0, keepdims=True)

    def softmax_step(j, s_buf, p_buf):
        for h in range(N_HEADS):
            m_old = m_ref[h]
            m_new = jnp.maximum(m_old, mx_ref[j % 2, h])
            for rows in chunks:
                p_buf[h, rows, :] = jnp.exp2(s_buf[h, rows, :] - m_new).astype(p_buf.dtype)
            alpha_ref[h] = jnp.exp2(m_old - m_new)
            m_ref[h] = m_new

    def accumulate(j, p_buf):
        jc = jnp.clip(j, 0, n)
        for h in range(N_HEADS):
            vt = cvt_ref[jc, h * V_ROWS:(h + 1) * V_ROWS, :]
            acc_ref[h] = alpha_ref[h] * acc_ref[h] + jnp.dot(vt, p_buf[h], preferred_element_type=F32)

    masked_logits(0, sa_ref)

    def tile_pair(i, carry):
        j = 2 * i
        accumulate(j - 1, pb_ref)
        masked_logits(j + 1, sb_ref)
        softmax_step(j, sa_ref, pa_ref)
        masked_logits(j + 2, sa_ref)
        accumulate(j, pa_ref)
        softmax_step(j + 1, sb_ref, pb_ref)
        return carry

    lax.fori_loop(0, n_pairs, tile_pair, 0)
    accumulate(2 * n_pairs - 1, pb_ref)

    out_t = jnp.concatenate([acc_ref[h, 0:HEAD_DIM, :] * (1.0 / acc_ref[h, HEAD_DIM:HEAD_DIM + 1, :])
                             for h in range(N_HEADS)], axis=0)
    o_ref[...] = (jnp.transpose(out_t) * gate_ref[...]).astype(o_ref.dtype)


def _dsa_attention(cqk4, cvt, ik4, iq, iw, gates, t5_rel, tri, bsz, seq):
    nt = seq // TILE
    assert nt % 2 == 0
    sdt = cqk4.dtype
    return pl.pallas_call(
        _dsa_kernel,
        grid=(bsz, nt),
        in_specs=[_resident((None, nt, TILE, 512), lambda b, i: (b, 0, 0, 0)),
                  _resident((None, nt, N_HEADS * V_ROWS, TILE), lambda b, i: (b, 0, 0, 0)),
                  _resident((None, nt, TILE, 256), lambda b, i: (b, 0, 0, 0)),
                  pl.BlockSpec((TILE, 512), lambda b, i: (b * nt + i, 0)),
                  pl.BlockSpec((TILE, 128), lambda b, i: (b * nt + i, 0)),
                  pl.BlockSpec((TILE, GROUP_W), lambda b, i: (b * nt + i, 2)),
                  _resident((N_HEADS, 2 * TILE, TILE), lambda b, i: (0, 0, 0)),
                  _resident((TILE, TILE), lambda b, i: (0, 0))],
        out_specs=pl.BlockSpec((TILE, GROUP_W), lambda b, i: (b * nt + i, 0)),
        out_shape=jax.ShapeDtypeStruct((bsz * seq, GROUP_W), sdt),
        scratch_shapes=[pltpu.VMEM((nt, TILE, TILE), I32),
                        pltpu.VMEM((32, nt * SUBLANES, TILE), I32),
                        pltpu.VMEM((nt * SUBLANES, TILE), I32),
                        pltpu.VMEM((nt + 2 * N_HEADS + 1, TILE, TILE), F32),
                        pltpu.VMEM((IDX_HEADS, TILE), F32),
                        pltpu.VMEM((N_HEADS, TILE, GROUP_W), sdt),
                        pltpu.VMEM((IDX_HEADS, TILE, TILE), F32),
                        pltpu.VMEM((IDX_HEADS, TILE, TILE), F32),
                        pltpu.VMEM((N_HEADS, TILE, TILE), F32),
                        pltpu.VMEM((N_HEADS, TILE, TILE), F32),
                        pltpu.VMEM((N_HEADS, TILE, TILE), sdt),
                        pltpu.VMEM((N_HEADS, TILE, TILE), sdt),
                        pltpu.VMEM((N_HEADS, 1, TILE), F32),
                        pltpu.VMEM((2, N_HEADS, 1, TILE), F32),
                        pltpu.VMEM((N_HEADS, 1, TILE), F32),
                        pltpu.VMEM((N_HEADS, V_ROWS, TILE), F32)],
        compiler_params=_cparams(("parallel", "arbitrary")),
        name="dsa_attention",
    )(cqk4, cvt, ik4, iq, iw, gates, t5_rel, tri)


def _outproj_kernel(h_ref, ya_ref, yb_ref, yc_ref, ym_ref, w_ref, b_ref, g_ref, beta_ref, o_ref):
    y = b_ref[...]
    for idx, part in enumerate((ya_ref, yb_ref, yc_ref, ym_ref)):
        y = y + jnp.dot(part[...], w_ref[idx * GROUP_W:(idx + 1) * GROUP_W, :],
                        preferred_element_type=F32)
    o_ref[...] = _layer_norm(DEEPNORM_ALPHA * h_ref[...] + y, g_ref[...], beta_ref[...])


def _out_projection(h, ya, yb, yc, ym, w_out, b_out, g, beta):
    m = h.shape[0]
    part = pl.BlockSpec((ROW_TILE, GROUP_W), lambda i: (i, 0))
    vec = pl.BlockSpec((1, D_MODEL), lambda i: (0, 0))
    return pl.pallas_call(
        _outproj_kernel,
        grid=(m // ROW_TILE,),
        in_specs=[pl.BlockSpec((ROW_TILE, D_MODEL), lambda i: (i, 0)), part, part, part, part,
                  pl.BlockSpec((D_MODEL, D_MODEL), lambda i: (0, 0)), vec, vec, vec],
        out_specs=pl.BlockSpec((ROW_TILE, D_MODEL), lambda i: (i, 0)),
        out_shape=jax.ShapeDtypeStruct((m, D_MODEL), F32),
        compiler_params=_cparams(("parallel",)),
        name="out_projection",
    )(h, ya, yb, yc, ym, w_out, b_out, g, beta)


def _prep_in_weights(w_in, b_in, sdt):
    sp = np.cumsum((0, 256, 256, 256, 256, 256, 256, 256, 256, 256, 256, 256, 512, 64, 8, 256, 256))
    names = ("a_u", "a_v", "a_g", "bq", "bk", "bv", "bg", "cq", "ck", "cv", "cg", "iq", "ik", "iw", "mq", "mg")
    col = {nm: slice(int(sp[i]), int(sp[i + 1])) for i, nm in enumerate(names)}
    qs = HEAD_DIM ** -0.5

    def build(t):
        z = lambda width: jnp.zeros((t.shape[0], width), t.dtype)
        ik = t[:, col["ik"]]
        main = [t[:, col["a_u"]], t[:, col["a_v"]],
                t[:, col["bq"]] * qs, t[:, col["bk"]], t[:, col["bv"]],
                t[:, col["cq"]] * (qs * LOG2E), t[:, col["ck"]],
                t[:, col["mq"]] * qs,
                t[:, col["a_g"]], t[:, col["bg"]], t[:, col["cg"]], t[:, col["mg"]]]
        idx = [t[:, col["iq"]], ik, z(64), z(64), ik, t[:, col["iw"]], z(120)]
        return jnp.concatenate(main, axis=1), jnp.concatenate(idx, axis=1), t[:, col["cv"]]

    w_main, w_idx, w_cv = build(w_in)
    b_main, b_idx, b_cv = build(b_in.reshape(1, -1))
    return (w_main.astype(sdt), b_main, w_idx, b_idx, jnp.transpose(w_cv).astype(sdt), jnp.transpose(b_cv))


def _toeplitz(g, n_rows, n_cols):
    span = n_rows + n_cols - 1
    lead = g.shape[:-1]
    period = jnp.concatenate([g, jnp.zeros(lead + (1,), g.dtype)], axis=-1)
    flat = jnp.tile(period, (1,) * len(lead) + (n_rows,))[..., :n_rows * span]
    return flat.reshape(lead + (n_rows, span))[..., n_rows - 1:n_rows - 1 + n_cols]


def _band_bias_table(b_rel):
    m = np.arange(4 * TILE - 1)
    rel_idx = np.clip(3 * TILE - 1 - m, -REL_CLIP, REL_CLIP) + REL_CLIP
    bias = _toeplitz(b_rel[:, jnp.asarray(rel_idx)], TILE, 3 * TILE)
    i = np.arange(TILE)[:, None]
    j = np.arange(3 * TILE)[None, :]
    chunk_diff = i // CHUNK - j // CHUNK + B_PREV_CHUNKS
    allowed = (chunk_diff >= 0) & (chunk_diff <= B_PREV_CHUNKS)
    return jnp.where(jnp.asarray(allowed)[None], bias, NEG_INF).astype(F32)


def _t5_bucket(rel):
    nb = T5_BUCKETS // 2
    max_exact = nb // 2
    ret = jnp.where(rel > 0, nb, 0)
    n = jnp.abs(rel)
    nf = jnp.maximum(n, 1).astype(F32)
    large = max_exact + (jnp.log(nf / max_exact) / math.log(T5_MAX_DIST / max_exact)
                         * (nb - max_exact)).astype(I32)
    large = jnp.minimum(large, nb - 1)
    return ret + jnp.where(n < max_exact, n, large)


def _t5_rel_table(t5_table):
    m = jnp.arange(3 * TILE - 1, dtype=I32)
    near = t5_table[_t5_bucket(TILE - 1 - m)]
    far = t5_table[_t5_bucket(jnp.full((1,), -(2 * TILE + 1), I32))]
    return _toeplitz(jnp.transpose(near - far) * LOG2E, 2 * TILE, TILE).astype(F32)


def kernel(x, mem, ln_in_g, ln_in_b, w_in, b_in, a_ln_g, a_ln_b, a_ws, a_bs, b_rel, t5_table,
           w_mem_kv, w_out, b_out, ln_g, ln_b):
    bsz, seq, _ = x.shape
    assert seq % ROW_TILE == 0 and x.shape[2] == D_MODEL
    m = bsz * seq
    nt = seq // TILE

    cpos = np.arange(A_BLOCK) // CHUNK
    a_mask = jnp.asarray(cpos[None, :] <= cpos[:, None])
    t5_rel = _t5_rel_table(t5_table)
    tri = jnp.asarray(np.tril(np.ones((TILE, TILE), np.float32), -1)).astype(BF16)

    h = x.reshape(m, D_MODEL)
    for l in range(DEPTH):
        sdt = BF16 if l == DEPTH - 1 else F32
        w_main, b_main, w_idx, b_idx, wvt, bvt = _prep_in_weights(w_in[l], b_in[l], sdt)
        proj = list(_in_projection(h, w_main, b_main, w_idx, b_idx, wvt, bvt, bsz, seq,
                                   input_norm=(ln_in_g, ln_in_b) if l == 0 else None))
        if l == 0:
            h = proj.pop()
        uv, bqkv, cqk, cvt, iq, ik, iw, mq, gates = proj

        ws_masked = jnp.where(a_mask[None], a_ws[l], 0.0).astype(sdt)
        bs_lanes = jnp.repeat(jnp.transpose(a_bs[l]), HEAD_DIM, axis=1)
        ya = _gmlp(uv, gates, a_ln_g[l].reshape(1, -1), a_ln_b[l].reshape(1, -1), ws_masked, bs_lanes)

        yb = _band_attention(bqkv.reshape(bsz, nt, TILE, 768), gates, _band_bias_table(b_rel[l]), bsz, seq)

        yc = _dsa_attention(cqk.reshape(bsz, nt, TILE, 512), cvt, ik.reshape(bsz, nt, TILE, 256),
                            iq, iw, gates, t5_rel, tri, bsz, seq)

        km, vm = _memory_kv(mem, w_mem_kv[l].astype(sdt))
        ym = _memory_attention(mq, km, vm, gates, bsz, seq)

        h = _out_projection(h, ya, yb, yc, ym, w_out[l].astype(sdt), b_out[l].reshape(1, -1),
                            ln_g[l].reshape(1, -1), ln_b[l].reshape(1, -1))
    return h.reshape(bsz, seq, D_MODEL)
```

```python
import functools
import math

import numpy as np
import jax
import jax.numpy as jnp
from jax import lax
from jax.experimental import pallas as pl
from jax.experimental.pallas import tpu as pltpu

F32 = jnp.float32
BF16 = jnp.bfloat16
I32 = jnp.int32

D_MODEL = 1024
CHUNK = 64
HEAD_DIM = 64
GROUP_W = 256
N_HEADS = GROUP_W // HEAD_DIM
A_BLOCK = 128
B_PREV_CHUNKS = 8
REL_CLIP = 128
IDX_HEADS = 8
IDX_DIM = 64
TOPK = 256
T5_BUCKETS = 32
T5_MAX_DIST = 128
N_MEM = 256
DEPTH = 2
DEEPNORM_ALPHA = (2 * DEPTH) ** 0.25
LN_EPS = 1e-5

TILE = 256
ROW_TILE = 512
VMEM_LIMIT = 56 * 1024 * 1024
NEG_INF = float("-inf")
SIGN_BIT = -2 ** 31
SUBLANES = 8
GROUP_TILES = 8
V_ROWS = HEAD_DIM + 16
ROW_CHUNK = 64
LOG2E = 1.4426950408889634

C_UV = (0, 512)
C_BQKV = (512, 1280)
C_CQK = (1280, 1792)
C_MQ = (1792, 2048)
C_GATE = (2048, 3072)
N_MAIN = 3072
C_IQ = (0, 512)
C_IK = (512, 768)
C_IW = (768, 896)
N_IDX = 896


def _cparams(sem):
    return pltpu.CompilerParams(dimension_semantics=sem, vmem_limit_bytes=VMEM_LIMIT)


def _resident(block_shape, index_map):
    return pl.BlockSpec(block_shape, index_map, pipeline_mode=pl.Buffered(1))


def _nt_dot(a, b):
    return lax.dot_general(a, b, (((1,), (1,)), ((), ())), preferred_element_type=F32)


def _gelu_tanh(x):
    return 0.5 * x * (1.0 + jnp.tanh(math.sqrt(2.0 / math.pi) * (x + 0.044715 * (x * x * x))))


def _silu(x):
    return x * (1.0 / (1.0 + jnp.exp(-x)))


def _layer_norm(x, g, b):
    mu = jnp.mean(x, axis=-1, keepdims=True)
    xc = x - mu
    var = jnp.mean(xc * xc, axis=-1, keepdims=True)
    return xc * lax.rsqrt(var + LN_EPS) * g + b


def _inproj_kernel(*refs, normalise_input):
    if normalise_input:
        h_ref, lng_ref, lnb_ref, *refs = refs
        *refs, hout_ref = refs
    else:
        h_ref, *refs = refs
    (w_ref, b_ref, widx_ref, bidx_ref, wvt_ref, bvt_ref,
     uv_ref, bqkv_ref, cqk_ref, cvt_ref, iq_ref, ik_ref, iw_ref, mq_ref, gate_ref) = refs
    x = h_ref[...]
    if normalise_input:
        x = _layer_norm(x, lng_ref[...], lnb_ref[...])
        hout_ref[...] = x
    xm = x.astype(w_ref.dtype)

    def seg(c):
        return jnp.dot(xm, w_ref[:, c[0]:c[1]], preferred_element_type=F32) + b_ref[:, c[0]:c[1]]

    def idx_seg(c):
        return jnp.dot(x, widx_ref[:, c[0]:c[1]], preferred_element_type=F32) + bidx_ref[:, c[0]:c[1]]

    uv_ref[...] = _gelu_tanh(seg(C_UV))
    bqkv_ref[...] = seg(C_BQKV).astype(bqkv_ref.dtype)
    cqk_ref[...] = seg(C_CQK).astype(cqk_ref.dtype)
    mq_ref[...] = seg(C_MQ).astype(mq_ref.dtype)
    gate_ref[...] = _silu(seg(C_GATE))
    iq_ref[...] = idx_seg(C_IQ)
    ik_ref[...] = idx_seg(C_IK)
    iw_ref[...] = idx_seg(C_IW)
    for t in range(ROW_TILE // TILE):
        vt = (_nt_dot(wvt_ref[...], xm[t * TILE:(t + 1) * TILE, :]) + bvt_ref[...]).astype(cvt_ref.dtype)
        for h in range(N_HEADS):
            cvt_ref[t, h * V_ROWS:h * V_ROWS + HEAD_DIM, :] = vt[h * HEAD_DIM:(h + 1) * HEAD_DIM, :]
            cvt_ref[t, h * V_ROWS + HEAD_DIM:(h + 1) * V_ROWS, :] = jnp.ones((V_ROWS - HEAD_DIM, TILE), cvt_ref.dtype)


def _in_projection(h, w_main, b_main, w_idx, b_idx, wvt, bvt, bsz, seq, input_norm=None):
    m = h.shape[0]
    sdt = w_main.dtype
    steps_per_batch = seq // ROW_TILE
    tiles_per_step = ROW_TILE // TILE

    def rows(width, dtype):
        return (pl.BlockSpec((ROW_TILE, width), lambda i: (i, 0)),
                jax.ShapeDtypeStruct((m, width), dtype))

    outs = [rows(512, F32), rows(768, sdt), rows(512, sdt),
            (pl.BlockSpec((None, tiles_per_step, N_HEADS * V_ROWS, TILE),
                          lambda i: (i // steps_per_batch, i % steps_per_batch, 0, 0)),
             jax.ShapeDtypeStruct((bsz, seq // TILE, N_HEADS * V_ROWS, TILE), sdt)),
            rows(512, F32), rows(256, F32), rows(128, F32), rows(256, sdt), rows(1024, F32)]
    vec = pl.BlockSpec((1, D_MODEL), lambda i: (0, 0))
    norm_specs, norm_args = [], []
    if input_norm is not None:
        outs.append(rows(D_MODEL, F32))
        norm_specs = [vec, vec]
        norm_args = [p.reshape(1, D_MODEL) for p in input_norm]
    return pl.pallas_call(
        functools.partial(_inproj_kernel, normalise_input=input_norm is not None),
        grid=(m // ROW_TILE,),
        in_specs=[pl.BlockSpec((ROW_TILE, D_MODEL), lambda i: (i, 0)), *norm_specs,
                  _resident((D_MODEL, N_MAIN), lambda i: (0, 0)),
                  pl.BlockSpec((1, N_MAIN), lambda i: (0, 0)),
                  _resident((D_MODEL, N_IDX), lambda i: (0, 0)),
                  pl.BlockSpec((1, N_IDX), lambda i: (0, 0)),
                  _resident((GROUP_W, D_MODEL), lambda i: (0, 0)),
                  pl.BlockSpec((GROUP_W, 1), lambda i: (0, 0))],
        out_specs=[o[0] for o in outs],
        out_shape=[o[1] for o in outs],
        compiler_params=_cparams(("parallel",)),
        name="in_projection",
    )(h, *norm_args, w_main, b_main, w_idx, b_idx, wvt, bvt)


def _gmlp_kernel(uv_ref, gate_ref, lng_ref, lnb_ref, ws_ref, bs_ref, o_ref):
    lane_group = lax.broadcasted_iota(I32, (A_BLOCK, GROUP_W), 1) // HEAD_DIM
    for blk in range(ROW_TILE // A_BLOCK):
        r = slice(blk * A_BLOCK, (blk + 1) * A_BLOCK)
        u = uv_ref[r, 0:GROUP_W]
        v = _layer_norm(uv_ref[r, GROUP_W:2 * GROUP_W], lng_ref[...], lnb_ref[...])
        vb = v.astype(ws_ref.dtype)
        mixed = bs_ref[...]
        for g in range(N_HEADS):
            mg = jnp.dot(ws_ref[g], vb, preferred_element_type=F32)
            mixed = mixed + jnp.where(lane_group == g, mg, 0.0)
        o_ref[r, :] = ((u * mixed) * gate_ref[r, :]).astype(o_ref.dtype)


def _gmlp(uv, gates, lng, lnb, ws_masked, bs_lanes):
    m = uv.shape[0]
    return pl.pallas_call(
        _gmlp_kernel,
        grid=(m // ROW_TILE,),
        in_specs=[pl.BlockSpec((ROW_TILE, 512), lambda i: (i, 0)),
                  pl.BlockSpec((ROW_TILE, GROUP_W), lambda i: (i, 0)),
                  pl.BlockSpec((1, GROUP_W), lambda i: (0, 0)),
                  pl.BlockSpec((1, GROUP_W), lambda i: (0, 0)),
                  pl.BlockSpec((N_HEADS, A_BLOCK, A_BLOCK), lambda i: (0, 0, 0)),
                  pl.BlockSpec((A_BLOCK, GROUP_W), lambda i: (0, 0))],
        out_specs=pl.BlockSpec((ROW_TILE, GROUP_W), lambda i: (i, 0)),
        out_shape=jax.ShapeDtypeStruct((m, GROUP_W), ws_masked.dtype),
        compiler_params=_cparams(("parallel",)),
        name="gmlp",
    )(uv, gates, lng, lnb, ws_masked, bs_lanes)


def _band_kernel(qkv_ref, gate_ref, bias_ref, o_ref):
    i = pl.program_id(1)
    q = qkv_ref[i, :, 0:GROUP_W]
    lane_head = lax.broadcasted_iota(I32, (TILE, GROUP_W), 1) // HEAD_DIM
    ks, vs, slabs = [], [], []
    for t, d in enumerate((2, 1, 0)):
        kb = i - d
        kbc = jnp.maximum(kb, 0)
        ks.append(qkv_ref[kbc, :, GROUP_W:2 * GROUP_W])
        vs.append(qkv_ref[kbc, :, 2 * GROUP_W:3 * GROUP_W])
        slabs.append(jnp.where(kb >= 0, t, 3))
    out = jnp.zeros((TILE, GROUP_W), F32)
    for h in range(N_HEADS):
        qh = jnp.where(lane_head == h, q, jnp.zeros_like(q))
        ss = [_nt_dot(qh, ks[t]) + bias_ref[h, slabs[t]] for t in range(3)]
        mx = jnp.maximum(jnp.maximum(jnp.max(ss[0], axis=1, keepdims=True),
                                     jnp.max(ss[1], axis=1, keepdims=True)),
                         jnp.max(ss[2], axis=1, keepdims=True))
        den = jnp.zeros((TILE, 1), F32)
        pv = jnp.zeros((TILE, GROUP_W), F32)
        for t in range(3):
            p = jnp.exp2(ss[t] - mx)
            den = den + jnp.sum(p, axis=1, keepdims=True)
            pv = pv + jnp.dot(p.astype(vs[t].dtype), vs[t], preferred_element_type=F32)
        out = jnp.where(lane_head == h, pv * (1.0 / den), out)
    o_ref[...] = (out * gate_ref[...]).astype(o_ref.dtype)


def _band_attention(bqkv4, gates, bias, bsz, seq):
    nt = seq // TILE
    return pl.pallas_call(
        _band_kernel,
        grid=(bsz, nt),
        in_specs=[pl.BlockSpec((None, nt, TILE, 768), lambda b, i: (b, 0, 0, 0)),
                  pl.BlockSpec((TILE, GROUP_W), lambda b, i: (b * nt + i, 1)),
                  _resident((N_HEADS, 4, TILE, TILE), lambda b, i: (0, 0, 0, 0))],
        out_specs=pl.BlockSpec((TILE, GROUP_W), lambda b, i: (b * nt + i, 0)),
        out_shape=jax.ShapeDtypeStruct((bsz * seq, GROUP_W), bqkv4.dtype),
        compiler_params=_cparams(("parallel", "arbitrary")),
        name="band_attention",
    )(bqkv4, gates, bias)


def _memkv_kernel(mem_ref, w_ref, km_ref, vm_ref):
    kv = jnp.dot(mem_ref[...].astype(w_ref.dtype), w_ref[...], preferred_element_type=F32)
    km_ref[...] = kv[:, 0:GROUP_W].astype(km_ref.dtype)
    vm_ref[...] = kv[:, GROUP_W:2 * GROUP_W].astype(vm_ref.dtype)


def _memory_kv(mem, w_kv):
    bsz, n_mem, _ = mem.shape
    blk = pl.BlockSpec((None, n_mem, GROUP_W), lambda b: (b, 0, 0))
    shp = jax.ShapeDtypeStruct((bsz, n_mem, GROUP_W), w_kv.dtype)
    return pl.pallas_call(
        _memkv_kernel,
        grid=(bsz,),
        in_specs=[pl.BlockSpec((None, n_mem, D_MODEL), lambda b: (b, 0, 0)),
                  pl.BlockSpec((D_MODEL, 2 * GROUP_W), lambda b: (0, 0))],
        out_specs=[blk, blk],
        out_shape=[shp, shp],
        compiler_params=_cparams(("parallel",)),
        name="memory_kv",
    )(mem, w_kv)


def _memattn_kernel(q_ref, km_ref, vm_ref, gate_ref, o_ref):
    q = q_ref[...]
    lane_head = lax.broadcasted_iota(I32, (ROW_TILE, GROUP_W), 1) // HEAD_DIM
    out = jnp.zeros((ROW_TILE, GROUP_W), F32)
    for h in range(N_HEADS):
        qh = jnp.where(lane_head == h, q, jnp.zeros_like(q))
        s = _nt_dot(qh, km_ref[...])
        p = jnp.exp2(s - jnp.max(s, axis=1, keepdims=True))
        den = jnp.sum(p, axis=1, keepdims=True)
        pv = jnp.dot(p.astype(vm_ref.dtype), vm_ref[...], preferred_element_type=F32)
        out = jnp.where(lane_head == h, pv * (1.0 / den), out)
    o_ref[...] = (out * gate_ref[...]).astype(o_ref.dtype)


def _memory_attention(mq, km, vm, gates, bsz, seq):
    steps = seq // ROW_TILE
    n_mem = km.shape[1]
    return pl.pallas_call(
        _memattn_kernel,
        grid=(bsz, steps),
        in_specs=[pl.BlockSpec((ROW_TILE, GROUP_W), lambda b, i: (b * steps + i, 0)),
                  pl.BlockSpec((None, n_mem, GROUP_W), lambda b, i: (b, 0, 0)),
                  pl.BlockSpec((None, n_mem, GROUP_W), lambda b, i: (b, 0, 0)),
                  pl.BlockSpec((ROW_TILE, GROUP_W), lambda b, i: (b * steps + i, 3))],
        out_specs=pl.BlockSpec((ROW_TILE, GROUP_W), lambda b, i: (b * steps + i, 0)),
        out_shape=jax.ShapeDtypeStruct((bsz * seq, GROUP_W), mq.dtype),
        compiler_params=_cparams(("parallel", "arbitrary")),
        name="memory_attention",
    )(mq, km, vm, gates)


def _dsa_kernel(cqk_ref, cvt_ref, ik_ref, iq_ref, iw_ref, gate_ref, t5_ref, tri_ref, o_ref,
                key_ref, planes_ref, live_ref, am_ref, iwt_ref, qh_ref, lg_ref, lgb_ref, sa_ref, sb_ref,
                pa_ref, pb_ref, m_ref, mx_ref, alpha_ref, acc_ref):
    n = pl.program_id(1)
    nt = key_ref.shape[0]
    q0 = n * TILE
    n_pairs = n // 2 + 1
    score_scale = (IDX_DIM ** -0.5) * (IDX_HEADS ** -0.5)
    chunks = [slice(c * ROW_CHUNK, (c + 1) * ROW_CHUNK) for c in range(TILE // ROW_CHUNK)]

    q_limit = ((q0 + lax.broadcasted_iota(I32, (1, TILE), 1)) // CHUNK + 1) * CHUNK
    chunk_pos = lax.broadcasted_iota(I32, (ROW_CHUNK, 1), 0)

    def admissible(rows):
        return q0 + rows.start + chunk_pos < q_limit

    iwt_ref[...] = jnp.transpose(iw_ref[...])[0:IDX_HEADS, :] * score_scale

    def index_logits(j, buf):
        for p in range(IDX_HEADS // 2):
            for parity in range(2):
                buf[2 * p + parity] = _nt_dot(ik_ref[j, :, parity * 128:(parity + 1) * 128],
                                              iq_ref[:, p * 128:(p + 1) * 128])

    def index_keys(j, buf, diagonal):
        for rows in chunks:
            sc = None
            for h in range(IDX_HEADS):
                term = jnp.maximum(buf[h, rows, :], 0.0) * iwt_ref[h:h + 1, :]
                sc = term if sc is None else sc + term
            if diagonal:
                sc = jnp.where(admissible(rows), sc, NEG_INF)
            bits = pltpu.bitcast(sc, I32)
            sign = lax.shift_right_arithmetic(bits, 31)
            key_ref[j, rows, :] = (bits ^ (sign & 0x7FFFFFFF)) - sign
        for lanes in (slice(0, 128), slice(128, 256)):
            x = [key_ref[j, SUBLANES * i:SUBLANES * (i + 1), lanes] for i in range(32)]
            step, keep = 16, 0x0000FFFF
            while step:
                k = 0
                while k < 32:
                    t = (x[k] ^ lax.shift_right_logical(x[k + step], step)) & keep
                    x[k] = x[k] ^ t
                    x[k + step] = x[k + step] ^ lax.shift_left(t, step)
                    k = (k + step + 1) & ~step
                step >>= 1
                keep = (keep ^ (keep << step)) & 0xFFFFFFFF
            dst = pl.ds(pl.multiple_of(j * SUBLANES, SUBLANES), SUBLANES)
            for b in range(32):
                plane = x[31 - b]
                planes_ref[b, dst, lanes] = ~plane if b == 31 else plane

    index_logits(0, lg_ref)

    def far_score_pair(i, carry):
        j = 2 * i
        index_logits(j + 1, lgb_ref)
        index_keys(j, lg_ref, False)
        index_logits(j + 2, lg_ref)
        index_keys(j + 1, lgb_ref, False)
        return carry

    lax.fori_loop(0, n // 2, far_score_pair, 0)

    @pl.when(n % 2 == 1)
    def _():
        index_logits(n, lgb_ref)
        index_keys(n - 1, lg_ref, False)
        index_keys(n, lgb_ref, True)

    @pl.when(n % 2 == 0)
    def _():
        index_keys(n, lg_ref, True)

    group_tiles = math.gcd(GROUP_TILES, nt)
    n_groups = n // group_tiles + 1
    group_rows = group_tiles * SUBLANES

    def clear_tail(t, carry):
        dst = pl.ds(pl.multiple_of(t * SUBLANES, SUBLANES), SUBLANES)
        planes_ref[:, dst, :] = jnp.zeros((32, SUBLANES, TILE), I32)
        live_ref[dst, :] = jnp.zeros((SUBLANES, TILE), I32)
        return carry

    lax.fori_loop(n + 1, n_groups * group_tiles, clear_tail, 0)

    def mark_live(t, carry):
        live_ref[pl.ds(pl.multiple_of(t * SUBLANES, SUBLANES), SUBLANES), :] = jnp.full((SUBLANES, TILE), -1, I32)
        return carry

    lax.fori_loop(0, n + 1, mark_live, 0)

    def group(g):
        return pl.ds(pl.multiple_of(g * group_rows, group_rows), group_rows)

    def count_live(b):
        def body(g, c):
            ones = live_ref[group(g), :]
            if b is not None:
                ones = ones & planes_ref[b, group(g), :]
            pc = lax.population_count(ones)
            for r in range(group_tiles):
                c = c + pc[r * SUBLANES:(r + 1) * SUBLANES, :]
            return c
        c = lax.fori_loop(0, n_groups, body, jnp.zeros((SUBLANES, TILE), I32))
        return jnp.sum(c, axis=0, keepdims=True)

    def bit_step(t, carry):
        need, thr_u, total = carry
        b = 31 - t
        take = jnp.where(total >= need, -1, 0)
        need = jnp.where(total >= need, need, need - total)
        thr_u = thr_u | (take & lax.shift_left(jnp.int32(1), b))
        b_next = jnp.maximum(b - 1, 0)

        def narrow_and_count(g, c):
            live = live_ref[group(g), :]
            ones = live & planes_ref[b, group(g), :]
            live = (live ^ ones) ^ (live & take)
            live_ref[group(g), :] = live
            pc = lax.population_count(live & planes_ref[b_next, group(g), :])
            for r in range(group_tiles):
                c = c + pc[r * SUBLANES:(r + 1) * SUBLANES, :]
            return c

        c = lax.fori_loop(0, n_groups, narrow_and_count, jnp.zeros((SUBLANES, TILE), I32))
        return need, thr_u, jnp.sum(c, axis=0, keepdims=True)

    quota, thr_u, _ = lax.fori_loop(0, 32, bit_step,
                                    (jnp.full((1, TILE), TOPK, I32), jnp.zeros((1, TILE), I32), count_live(31)))
    thr = thr_u ^ SIGN_BIT
    n_equal = count_live(None)
    ties_cut = jnp.max(n_equal - quota) > 0

    am_ref[nt + 2 * N_HEADS] = jnp.full((TILE, TILE), NEG_INF, F32)

    def write_masks(with_ties):
        quota_f = quota.astype(F32)

        def selected(j, rows, taken, diagonal):
            k = key_ref[j, rows, :]
            open_ = jnp.where(admissible(rows), 0.0, NEG_INF) if diagonal else 0.0
            if not with_ties:
                return jnp.where(k >= thr, open_, NEG_INF), taken
            eq = jnp.where(k == thr, 1.0, 0.0).astype(BF16)
            before = jnp.dot(tri_ref[rows.start:rows.stop, rows.start:rows.stop], eq,
                             preferred_element_type=F32) + taken
            tied = jnp.where(k == thr, jnp.where(before < quota_f, open_, NEG_INF), NEG_INF)
            return jnp.where(k > thr, open_, tied), taken + jnp.sum(eq.astype(F32), axis=0, keepdims=True)

        def far_mask(j, taken):
            for rows in chunks:
                am, taken = selected(j, rows, taken, False)
                am_ref[j, rows, :] = am
            return taken

        taken = lax.fori_loop(0, n - 1, far_mask, jnp.zeros((1, TILE), F32))

        def near_mask(j, d, taken):
            for rows in chunks:
                am, taken = selected(j, rows, taken, d == 1)
                for h in range(N_HEADS):
                    am_ref[nt + N_HEADS * d + h, rows, :] = am + t5_ref[h, d * TILE + rows.start:d * TILE + rows.stop, :]
            return taken

        prev = near_mask(jnp.maximum(n - 1, 0), 0, taken)
        near_mask(n, 1, jnp.where(n >= 1, prev, taken))

    @pl.when(ties_cut)
    def _():
        write_masks(True)

    @pl.when(jnp.logical_not(ties_cut))
    def _():
        write_masks(False)

    m_ref[...] = jnp.full(m_ref.shape, -1e30, F32)
    acc_ref[...] = jnp.zeros(acc_ref.shape, F32)
    alpha_ref[...] = jnp.ones(alpha_ref.shape, F32)
    pb_ref[...] = jnp.zeros(pb_ref.shape, pb_ref.dtype)
    q = cqk_ref[n, :, 0:GROUP_W]
    lane_head = lax.broadcasted_iota(I32, (TILE, GROUP_W), 1) // HEAD_DIM
    for h in range(N_HEADS):
        qh_ref[h] = jnp.where(lane_head == h, q, jnp.zeros_like(q))

    def masked_logits(j, s_buf):
        k = cqk_ref[jnp.minimum(j, n), :, GROUP_W:2 * GROUP_W]
        near = j - (n - 1)
        for h in range(N_HEADS):
            slab = jnp.where(j > n, nt + 2 * N_HEADS, jnp.where(near >= 0, nt + N_HEADS * near + h, j))
            s = _nt_dot(k, qh_ref[h]) + am_ref[slab]
            s_buf[h] = s
            mx_ref[j % 2, h] = jnp.max(s, axis=0, keepdims=True)

    def softmax_step(j, s_buf, p_buf):
        for h in range(N_HEADS):
            m_old = m_ref[h]
            m_new = jnp.maximum(m_old, mx_ref[j % 2, h])
            for rows in chunks:
                p_buf[h, rows, :] = jnp.exp2(s_buf[h, rows, :] - m_new).astype(p_buf.dtype)
            alpha_ref[h] = jnp.exp2(m_old - m_new)
            m_ref[h] = m_new

    def accumulate(j, p_buf):
        jc = jnp.clip(j, 0, n)
        for h in range(N_HEADS):
            vt = cvt_ref[jc, h * V_ROWS:(h + 1) * V_ROWS, :]
            acc_ref[h] = alpha_ref[h] * acc_ref[h] + jnp.dot(vt, p_buf[h], preferred_element_type=F32)

    masked_logits(0, sa_ref)

    def tile_pair(i, carry):
        j = 2 * i
        accumulate(j - 1, pb_ref)
        masked_logits(j + 1, sb_ref)
        softmax_step(j, sa_ref, pa_ref)
        masked_logits(j + 2, sa_ref)
        accumulate(j, pa_ref)
        softmax_step(j + 1, sb_ref, pb_ref)
        return carry

    lax.fori_loop(0, n_pairs, tile_pair, 0)
    accumulate(2 * n_pairs - 1, pb_ref)

    out_t = jnp.concatenate([acc_ref[h, 0:HEAD_DIM, :] * (1.0 / acc_ref[h, HEAD_DIM:HEAD_DIM + 1, :])
                             for h in range(N_HEADS)], axis=0)
    o_ref[...] = (jnp.transpose(out_t) * gate_ref[...]).astype(o_ref.dtype)


def _dsa_attention(cqk4, cvt, ik4, iq, iw, gates, t5_rel, tri, bsz, seq):
    nt = seq // TILE
    assert nt % 2 == 0
    sdt = cqk4.dtype
    return pl.pallas_call(
        _dsa_kernel,
        grid=(bsz, nt),
        in_specs=[_resident((None, nt, TILE, 512), lambda b, i: (b, 0, 0, 0)),
                  _resident((None, nt, N_HEADS * V_ROWS, TILE), lambda b, i: (b, 0, 0, 0)),
                  _resident((None, nt, TILE, 256), lambda b, i: (b, 0, 0, 0)),
                  pl.BlockSpec((TILE, 512), lambda b, i: (b * nt + i, 0)),
                  pl.BlockSpec((TILE, 128), lambda b, i: (b * nt + i, 0)),
                  pl.BlockSpec((TILE, GROUP_W), lambda b, i: (b * nt + i, 2)),
                  _resident((N_HEADS, 2 * TILE, TILE), lambda b, i: (0, 0, 0)),
                  _resident((TILE, TILE), lambda b, i: (0, 0))],
        out_specs=pl.BlockSpec((TILE, GROUP_W), lambda b, i: (b * nt + i, 0)),
        out_shape=jax.ShapeDtypeStruct((bsz * seq, GROUP_W), sdt),
        scratch_shapes=[pltpu.VMEM((nt, TILE, TILE), I32),
                        pltpu.VMEM((32, nt * SUBLANES, TILE), I32),
                        pltpu.VMEM((nt * SUBLANES, TILE), I32),
                        pltpu.VMEM((nt + 2 * N_HEADS + 1, TILE, TILE), F32),
                        pltpu.VMEM((IDX_HEADS, TILE), F32),
                        pltpu.VMEM((N_HEADS, TILE, GROUP_W), sdt),
                        pltpu.VMEM((IDX_HEADS, TILE, TILE), F32),
                        pltpu.VMEM((IDX_HEADS, TILE, TILE), F32),
                        pltpu.VMEM((N_HEADS, TILE, TILE), F32),
                        pltpu.VMEM((N_HEADS, TILE, TILE), F32),
                        pltpu.VMEM((N_HEADS, TILE, TILE), sdt),
                        pltpu.VMEM((N_HEADS, TILE, TILE), sdt),
                        pltpu.VMEM((N_HEADS, 1, TILE), F32),
                        pltpu.VMEM((2, N_HEADS, 1, TILE), F32),
                        pltpu.VMEM((N_HEADS, 1, TILE), F32),
                        pltpu.VMEM((N_HEADS, V_ROWS, TILE), F32)],
        compiler_params=_cparams(("parallel", "arbitrary")),
        name="dsa_attention",
    )(cqk4, cvt, ik4, iq, iw, gates, t5_rel, tri)


def _outproj_kernel(h_ref, ya_ref, yb_ref, yc_ref, ym_ref, w_ref, b_ref, g_ref, beta_ref, o_ref):
    y = b_ref[...]
    for idx, part in enumerate((ya_ref, yb_ref, yc_ref, ym_ref)):
        y = y + jnp.dot(part[...], w_ref[idx * GROUP_W:(idx + 1) * GROUP_W, :],
                        preferred_element_type=F32)
    o_ref[...] = _layer_norm(DEEPNORM_ALPHA * h_ref[...] + y, g_ref[...], beta_ref[...])


def _out_projection(h, ya, yb, yc, ym, w_out, b_out, g, beta):
    m = h.shape[0]
    part = pl.BlockSpec((ROW_TILE, GROUP_W), lambda i: (i, 0))
    vec = pl.BlockSpec((1, D_MODEL), lambda i: (0, 0))
    return pl.pallas_call(
        _outproj_kernel,
        grid=(m // ROW_TILE,),
        in_specs=[pl.BlockSpec((ROW_TILE, D_MODEL), lambda i: (i, 0)), part, part, part, part,
                  pl.BlockSpec((D_MODEL, D_MODEL), lambda i: (0, 0)), vec, vec, vec],
        out_specs=pl.BlockSpec((ROW_TILE, D_MODEL), lambda i: (i, 0)),
        out_shape=jax.ShapeDtypeStruct((m, D_MODEL), F32),
        compiler_params=_cparams(("parallel",)),
        name="out_projection",
    )(h, ya, yb, yc, ym, w_out, b_out, g, beta)


def _prep_in_weights(w_in, b_in, sdt):
    sp = np.cumsum((0, 256, 256, 256, 256, 256, 256, 256, 256, 256, 256, 256, 512, 64, 8, 256, 256))
    names = ("a_u", "a_v", "a_g", "bq", "bk", "bv", "bg", "cq", "ck", "cv", "cg", "iq", "ik", "iw", "mq", "mg")
    col = {nm: slice(int(sp[i]), int(sp[i + 1])) for i, nm in enumerate(names)}
    qs = HEAD_DIM ** -0.5

    def build(t):
        z = lambda width: jnp.zeros((t.shape[0], width), t.dtype)
        ik = t[:, col["ik"]]
        main = [t[:, col["a_u"]], t[:, col["a_v"]],
                t[:, col["bq"]] * (qs * LOG2E), t[:, col["bk"]], t[:, col["bv"]],
                t[:, col["cq"]] * (qs * LOG2E), t[:, col["ck"]],
                t[:, col["mq"]] * (qs * LOG2E),
                t[:, col["a_g"]], t[:, col["bg"]], t[:, col["cg"]], t[:, col["mg"]]]
        idx = [t[:, col["iq"]], ik, z(64), z(64), ik, t[:, col["iw"]], z(120)]
        return jnp.concatenate(main, axis=1), jnp.concatenate(idx, axis=1), t[:, col["cv"]]

    w_main, w_idx, w_cv = build(w_in)
    b_main, b_idx, b_cv = build(b_in.reshape(1, -1))
    return (w_main.astype(sdt), b_main, w_idx, b_idx, jnp.transpose(w_cv).astype(sdt), jnp.transpose(b_cv))


def _toeplitz(g, n_rows, n_cols):
    span = n_rows + n_cols - 1
    lead = g.shape[:-1]
    period = jnp.concatenate([g, jnp.zeros(lead + (1,), g.dtype)], axis=-1)
    flat = jnp.tile(period, (1,) * len(lead) + (n_rows,))[..., :n_rows * span]
    return flat.reshape(lead + (n_rows, span))[..., n_rows - 1:n_rows - 1 + n_cols]


def _band_bias_table(b_rel):
    m = np.arange(4 * TILE - 1)
    rel_idx = np.clip(3 * TILE - 1 - m, -REL_CLIP, REL_CLIP) + REL_CLIP
    bias = _toeplitz(b_rel[:, jnp.asarray(rel_idx)] * LOG2E, TILE, 3 * TILE)
    i = np.arange(TILE)[:, None]
    j = np.arange(3 * TILE)[None, :]
    chunk_diff = i // CHUNK - j // CHUNK + B_PREV_CHUNKS
    allowed = (chunk_diff >= 0) & (chunk_diff <= B_PREV_CHUNKS)
    bias = jnp.where(jnp.asarray(allowed)[None], bias, NEG_INF).astype(F32)
    slabs = [bias[:, None, :, t * TILE:(t + 1) * TILE] for t in range(3)]
    return jnp.concatenate(slabs + [jnp.full((N_HEADS, 1, TILE, TILE), NEG_INF, F32)], axis=1)


def _t5_bucket(rel):
    nb = T5_BUCKETS // 2
    max_exact = nb // 2
    ret = jnp.where(rel > 0, nb, 0)
    n = jnp.abs(rel)
    nf = jnp.maximum(n, 1).astype(F32)
    large = max_exact + (jnp.log(nf / max_exact) / math.log(T5_MAX_DIST / max_exact)
                         * (nb - max_exact)).astype(I32)
    large = jnp.minimum(large, nb - 1)
    return ret + jnp.where(n < max_exact, n, large)


def _t5_rel_table(t5_table):
    m = jnp.arange(3 * TILE - 1, dtype=I32)
    near = t5_table[_t5_bucket(TILE - 1 - m)]
    far = t5_table[_t5_bucket(jnp.full((1,), -(2 * TILE + 1), I32))]
    return _toeplitz(jnp.transpose(near - far) * LOG2E, 2 * TILE, TILE).astype(F32)


def kernel(x, mem, ln_in_g, ln_in_b, w_in, b_in, a_ln_g, a_ln_b, a_ws, a_bs, b_rel, t5_table,
           w_mem_kv, w_out, b_out, ln_g, ln_b):
    bsz, seq, _ = x.shape
    assert seq % ROW_TILE == 0 and x.shape[2] == D_MODEL
    m = bsz * seq
    nt = seq // TILE

    cpos = np.arange(A_BLOCK) // CHUNK
    a_mask = jnp.asarray(cpos[None, :] <= cpos[:, None])
    t5_rel = _t5_rel_table(t5_table)
    tri = jnp.asarray(np.tril(np.ones((TILE, TILE), np.float32), -1)).astype(BF16)

    h = x.reshape(m, D_MODEL)
    for l in range(DEPTH):
        sdt = BF16 if l == DEPTH - 1 else F32
        w_main, b_main, w_idx, b_idx, wvt, bvt = _prep_in_weights(w_in[l], b_in[l], sdt)
        proj = list(_in_projection(h, w_main, b_main, w_idx, b_idx, wvt, bvt, bsz, seq,
                                   input_norm=(ln_in_g, ln_in_b) if l == 0 else None))
        if l == 0:
            h = proj.pop()
        uv, bqkv, cqk, cvt, iq, ik, iw, mq, gates = proj

        ws_masked = jnp.where(a_mask[None], a_ws[l], 0.0).astype(sdt)
        bs_lanes = jnp.repeat(jnp.transpose(a_bs[l]), HEAD_DIM, axis=1)
        ya = _gmlp(uv, gates, a_ln_g[l].reshape(1, -1), a_ln_b[l].reshape(1, -1), ws_masked, bs_lanes)

        yb = _band_attention(bqkv.reshape(bsz, nt, TILE, 768), gates, _band_bias_table(b_rel[l]), bsz, seq)

        yc = _dsa_attention(cqk.reshape(bsz, nt, TILE, 512), cvt, ik.reshape(bsz, nt, TILE, 256),
                            iq, iw, gates, t5_rel, tri, bsz, seq)

        km, vm = _memory_kv(mem, w_mem_kv[l].astype(sdt))
        ym = _memory_attention(mq, km, vm, gates, bsz, seq)

        h = _out_projection(h, ya, yb, yc, ym, w_out[l].astype(sdt), b_out[l].reshape(1, -1),
                            ln_g[l].reshape(1, -1), ln_b[l].reshape(1, -1))
    return h.reshape(bsz, seq, D_MODEL)
```

```python
import functools
import math

import numpy as np
import jax
import jax.numpy as jnp
from jax import lax
from jax.experimental import pallas as pl
from jax.experimental.pallas import tpu as pltpu

F32 = jnp.float32
BF16 = jnp.bfloat16
I32 = jnp.int32

D_MODEL = 1024
CHUNK = 64
HEAD_DIM = 64
GROUP_W = 256
N_HEADS = GROUP_W // HEAD_DIM
A_BLOCK = 128
B_PREV_CHUNKS = 8
REL_CLIP = 128
IDX_HEADS = 8
IDX_DIM = 64
TOPK = 256
T5_BUCKETS = 32
T5_MAX_DIST = 128
N_MEM = 256
DEPTH = 2
DEEPNORM_ALPHA = (2 * DEPTH) ** 0.25
LN_EPS = 1e-5

TILE = 256
ROW_TILE = 512
MIX_TILE = 1024
VMEM_LIMIT = 56 * 1024 * 1024
NEG_INF = float("-inf")
SIGN_BIT = -2 ** 31
SUBLANES = 8
GROUP_TILES = 8
V_ROWS = HEAD_DIM + 16
ROW_CHUNK = 64
LOG2E = 1.4426950408889634

C_UV = (0, 512)
C_BQKV = (512, 1280)
C_CQK = (1280, 1792)
C_MQ = (1792, 2048)
C_GATE = (2048, 3072)
N_MAIN = 3072
C_IQ = (0, 512)
C_IK = (512, 768)
C_IW = (768, 896)
N_IDX = 896


def _cparams(sem):
    return pltpu.CompilerParams(dimension_semantics=sem, vmem_limit_bytes=VMEM_LIMIT)


def _resident(block_shape, index_map):
    return pl.BlockSpec(block_shape, index_map, pipeline_mode=pl.Buffered(1))


def _nt_dot(a, b):
    return lax.dot_general(a, b, (((1,), (1,)), ((), ())), preferred_element_type=F32)


def _gelu_tanh(x):
    return 0.5 * x * (1.0 + jnp.tanh(math.sqrt(2.0 / math.pi) * (x + 0.044715 * (x * x * x))))


def _silu(x):
    return x * (1.0 / (1.0 + jnp.exp(-x)))


def _layer_norm(x, g, b):
    mu = jnp.mean(x, axis=-1, keepdims=True)
    xc = x - mu
    var = jnp.mean(xc * xc, axis=-1, keepdims=True)
    return xc * lax.rsqrt(var + LN_EPS) * g + b


def _inproj_kernel(*refs, normalise_input):
    if normalise_input:
        h_ref, lng_ref, lnb_ref, *refs = refs
        *refs, hout_ref = refs
    else:
        h_ref, *refs = refs
    (w_ref, b_ref, widx_ref, bidx_ref, wvt_ref, bvt_ref,
     uv_ref, bqkv_ref, cqk_ref, cvt_ref, iq_ref, ik_ref, iw_ref, mq_ref, gate_ref) = refs
    x = h_ref[...]
    if normalise_input:
        x = _layer_norm(x, lng_ref[...], lnb_ref[...])
        hout_ref[...] = x
    xm = x.astype(w_ref.dtype)

    def seg(c):
        return jnp.dot(xm, w_ref[:, c[0]:c[1]], preferred_element_type=F32) + b_ref[:, c[0]:c[1]]

    def idx_seg(c):
        return jnp.dot(x, widx_ref[:, c[0]:c[1]], preferred_element_type=F32) + bidx_ref[:, c[0]:c[1]]

    uv_ref[...] = _gelu_tanh(seg(C_UV))
    bqkv_ref[...] = seg(C_BQKV).astype(bqkv_ref.dtype)
    cqk_ref[...] = seg(C_CQK).astype(cqk_ref.dtype)
    mq_ref[...] = seg(C_MQ).astype(mq_ref.dtype)
    gate_ref[...] = _silu(seg(C_GATE))
    iq_ref[...] = idx_seg(C_IQ)
    ik_ref[...] = idx_seg(C_IK)
    iw_ref[...] = idx_seg(C_IW)
    for t in range(ROW_TILE // TILE):
        vt = (_nt_dot(wvt_ref[...], xm[t * TILE:(t + 1) * TILE, :]) + bvt_ref[...]).astype(cvt_ref.dtype)
        for h in range(N_HEADS):
            cvt_ref[t, h * V_ROWS:h * V_ROWS + HEAD_DIM, :] = vt[h * HEAD_DIM:(h + 1) * HEAD_DIM, :]
            cvt_ref[t, h * V_ROWS + HEAD_DIM:(h + 1) * V_ROWS, :] = jnp.ones((V_ROWS - HEAD_DIM, TILE), cvt_ref.dtype)


def _in_projection(h, w_main, b_main, w_idx, b_idx, wvt, bvt, bsz, seq, input_norm=None):
    m = h.shape[0]
    sdt = w_main.dtype
    steps_per_batch = seq // ROW_TILE
    tiles_per_step = ROW_TILE // TILE

    def rows(width, dtype):
        return (pl.BlockSpec((ROW_TILE, width), lambda i: (i, 0)),
                jax.ShapeDtypeStruct((m, width), dtype))

    outs = [rows(512, F32), rows(768, sdt), rows(512, sdt),
            (pl.BlockSpec((None, tiles_per_step, N_HEADS * V_ROWS, TILE),
                          lambda i: (i // steps_per_batch, i % steps_per_batch, 0, 0)),
             jax.ShapeDtypeStruct((bsz, seq // TILE, N_HEADS * V_ROWS, TILE), sdt)),
            rows(512, F32), rows(256, F32), rows(128, F32), rows(256, sdt), rows(1024, F32)]
    vec = pl.BlockSpec((1, D_MODEL), lambda i: (0, 0))
    norm_specs, norm_args = [], []
    if input_norm is not None:
        outs.append(rows(D_MODEL, F32))
        norm_specs = [vec, vec]
        norm_args = [p.reshape(1, D_MODEL) for p in input_norm]
    return pl.pallas_call(
        functools.partial(_inproj_kernel, normalise_input=input_norm is not None),
        grid=(m // ROW_TILE,),
        in_specs=[pl.BlockSpec((ROW_TILE, D_MODEL), lambda i: (i, 0)), *norm_specs,
                  _resident((D_MODEL, N_MAIN), lambda i: (0, 0)),
                  pl.BlockSpec((1, N_MAIN), lambda i: (0, 0)),
                  _resident((D_MODEL, N_IDX), lambda i: (0, 0)),
                  pl.BlockSpec((1, N_IDX), lambda i: (0, 0)),
                  _resident((GROUP_W, D_MODEL), lambda i: (0, 0)),
                  pl.BlockSpec((GROUP_W, 1), lambda i: (0, 0))],
        out_specs=[o[0] for o in outs],
        out_shape=[o[1] for o in outs],
        compiler_params=_cparams(("parallel",)),
        name="in_projection",
    )(h, *norm_args, w_main, b_main, w_idx, b_idx, wvt, bvt)


def _gmlp_kernel(uv_ref, gate_ref, lng_ref, lnb_ref, ws_ref, bs_ref, o_ref):
    lane_group = lax.broadcasted_iota(I32, (A_BLOCK, GROUP_W), 1) // HEAD_DIM
    for blk in range(MIX_TILE // A_BLOCK):
        r = slice(blk * A_BLOCK, (blk + 1) * A_BLOCK)
        u = uv_ref[r, 0:GROUP_W]
        v = _layer_norm(uv_ref[r, GROUP_W:2 * GROUP_W], lng_ref[...], lnb_ref[...])
        vb = v.astype(ws_ref.dtype)
        mixed = bs_ref[...]
        for g in range(N_HEADS):
            mg = jnp.dot(ws_ref[g], vb, preferred_element_type=F32)
            mixed = mixed + jnp.where(lane_group == g, mg, 0.0)
        o_ref[r, :] = ((u * mixed) * gate_ref[r, :]).astype(o_ref.dtype)


def _gmlp(uv, gates, lng, lnb, ws_masked, bs_lanes):
    m = uv.shape[0]
    return pl.pallas_call(
        _gmlp_kernel,
        grid=(m // MIX_TILE,),
        in_specs=[pl.BlockSpec((MIX_TILE, 512), lambda i: (i, 0)),
                  pl.BlockSpec((MIX_TILE, GROUP_W), lambda i: (i, 0)),
                  pl.BlockSpec((1, GROUP_W), lambda i: (0, 0)),
                  pl.BlockSpec((1, GROUP_W), lambda i: (0, 0)),
                  pl.BlockSpec((N_HEADS, A_BLOCK, A_BLOCK), lambda i: (0, 0, 0)),
                  pl.BlockSpec((A_BLOCK, GROUP_W), lambda i: (0, 0))],
        out_specs=pl.BlockSpec((MIX_TILE, GROUP_W), lambda i: (i, 0)),
        out_shape=jax.ShapeDtypeStruct((m, GROUP_W), ws_masked.dtype),
        compiler_params=_cparams(("parallel",)),
        name="gmlp",
    )(uv, gates, lng, lnb, ws_masked, bs_lanes)


def _band_kernel(qkv_ref, gate_ref, bias_ref, o_ref):
    i = pl.program_id(1)
    q = qkv_ref[i, :, 0:GROUP_W]
    lane_head = lax.broadcasted_iota(I32, (TILE, GROUP_W), 1) // HEAD_DIM
    ks, vs, slabs = [], [], []
    for t, d in enumerate((2, 1, 0)):
        kb = i - d
        kbc = jnp.maximum(kb, 0)
        ks.append(qkv_ref[kbc, :, GROUP_W:2 * GROUP_W])
        vs.append(qkv_ref[kbc, :, 2 * GROUP_W:3 * GROUP_W])
        slabs.append(jnp.where(kb >= 0, t, 3))
    out = jnp.zeros((TILE, GROUP_W), F32)
    for h in range(N_HEADS):
        qh = jnp.where(lane_head == h, q, jnp.zeros_like(q))
        ss = [_nt_dot(qh, ks[t]) + bias_ref[h, slabs[t]] for t in range(3)]
        mx = jnp.maximum(jnp.maximum(jnp.max(ss[0], axis=1, keepdims=True),
                                     jnp.max(ss[1], axis=1, keepdims=True)),
                         jnp.max(ss[2], axis=1, keepdims=True))
        den = jnp.zeros((TILE, 1), F32)
        pv = jnp.zeros((TILE, GROUP_W), F32)
        for t in range(3):
            p = jnp.exp2(ss[t] - mx)
            den = den + jnp.sum(p, axis=1, keepdims=True)
            pv = pv + jnp.dot(p.astype(vs[t].dtype), vs[t], preferred_element_type=F32)
        out = jnp.where(lane_head == h, pv * (1.0 / den), out)
    o_ref[...] = (out * gate_ref[...]).astype(o_ref.dtype)


def _band_attention(bqkv4, gates, bias, bsz, seq):
    nt = seq // TILE
    return pl.pallas_call(
        _band_kernel,
        grid=(bsz, nt),
        in_specs=[pl.BlockSpec((None, nt, TILE, 768), lambda b, i: (b, 0, 0, 0)),
                  pl.BlockSpec((TILE, GROUP_W), lambda b, i: (b * nt + i, 1)),
                  _resident((N_HEADS, 4, TILE, TILE), lambda b, i: (0, 0, 0, 0))],
        out_specs=pl.BlockSpec((TILE, GROUP_W), lambda b, i: (b * nt + i, 0)),
        out_shape=jax.ShapeDtypeStruct((bsz * seq, GROUP_W), bqkv4.dtype),
        compiler_params=_cparams(("parallel", "arbitrary")),
        name="band_attention",
    )(bqkv4, gates, bias)


def _memkv_kernel(mem_ref, w_ref, km_ref, vm_ref):
    kv = jnp.dot(mem_ref[...].astype(w_ref.dtype), w_ref[...], preferred_element_type=F32)
    km_ref[...] = kv[:, 0:GROUP_W].astype(km_ref.dtype)
    vm_ref[...] = kv[:, GROUP_W:2 * GROUP_W].astype(vm_ref.dtype)


def _memory_kv(mem, w_kv):
    bsz, n_mem, _ = mem.shape
    blk = pl.BlockSpec((None, n_mem, GROUP_W), lambda b: (b, 0, 0))
    shp = jax.ShapeDtypeStruct((bsz, n_mem, GROUP_W), w_kv.dtype)
    return pl.pallas_call(
        _memkv_kernel,
        grid=(bsz,),
        in_specs=[pl.BlockSpec((None, n_mem, D_MODEL), lambda b: (b, 0, 0)),
                  pl.BlockSpec((D_MODEL, 2 * GROUP_W), lambda b: (0, 0))],
        out_specs=[blk, blk],
        out_shape=[shp, shp],
        compiler_params=_cparams(("parallel",)),
        name="memory_kv",
    )(mem, w_kv)


def _memattn_kernel(q_ref, km_ref, vm_ref, gate_ref, o_ref):
    q = q_ref[...]
    lane_head = lax.broadcasted_iota(I32, (MIX_TILE, GROUP_W), 1) // HEAD_DIM
    out = jnp.zeros((MIX_TILE, GROUP_W), F32)
    for h in range(N_HEADS):
        qh = jnp.where(lane_head == h, q, jnp.zeros_like(q))
        s = _nt_dot(qh, km_ref[...])
        p = jnp.exp2(s - jnp.max(s, axis=1, keepdims=True))
        den = jnp.sum(p, axis=1, keepdims=True)
        pv = jnp.dot(p.astype(vm_ref.dtype), vm_ref[...], preferred_element_type=F32)
        out = jnp.where(lane_head == h, pv * (1.0 / den), out)
    o_ref[...] = (out * gate_ref[...]).astype(o_ref.dtype)


def _memory_attention(mq, km, vm, gates, bsz, seq):
    steps = seq // MIX_TILE
    n_mem = km.shape[1]
    return pl.pallas_call(
        _memattn_kernel,
        grid=(bsz, steps),
        in_specs=[pl.BlockSpec((MIX_TILE, GROUP_W), lambda b, i: (b * steps + i, 0)),
                  pl.BlockSpec((None, n_mem, GROUP_W), lambda b, i: (b, 0, 0)),
                  pl.BlockSpec((None, n_mem, GROUP_W), lambda b, i: (b, 0, 0)),
                  pl.BlockSpec((MIX_TILE, GROUP_W), lambda b, i: (b * steps + i, 3))],
        out_specs=pl.BlockSpec((MIX_TILE, GROUP_W), lambda b, i: (b * steps + i, 0)),
        out_shape=jax.ShapeDtypeStruct((bsz * seq, GROUP_W), mq.dtype),
        compiler_params=_cparams(("parallel", "arbitrary")),
        name="memory_attention",
    )(mq, km, vm, gates)


def _dsa_kernel(cqk_ref, cvt_ref, ik_ref, iq_ref, iw_ref, gate_ref, t5_ref, tri_ref, o_ref,
                key_ref, planes_ref, live_ref, am_ref, iwt_ref, qh_ref, lg_ref, lgb_ref, sa_ref, sb_ref,
                pa_ref, pb_ref, m_ref, mx_ref, alpha_ref, acc_ref):
    n = pl.program_id(1)
    nt = key_ref.shape[0]
    q0 = n * TILE
    n_pairs = n // 2 + 1
    score_scale = (IDX_DIM ** -0.5) * (IDX_HEADS ** -0.5)
    chunks = [slice(c * ROW_CHUNK, (c + 1) * ROW_CHUNK) for c in range(TILE // ROW_CHUNK)]

    q_limit = ((q0 + lax.broadcasted_iota(I32, (1, TILE), 1)) // CHUNK + 1) * CHUNK
    chunk_pos = lax.broadcasted_iota(I32, (ROW_CHUNK, 1), 0)

    def admissible(rows):
        return q0 + rows.start + chunk_pos < q_limit

    iwt_ref[...] = jnp.transpose(iw_ref[...])[0:IDX_HEADS, :] * score_scale

    def index_logits(j, buf):
        for p in range(IDX_HEADS // 2):
            for parity in range(2):
                buf[2 * p + parity] = _nt_dot(ik_ref[j, :, parity * 128:(parity + 1) * 128],
                                              iq_ref[:, p * 128:(p + 1) * 128])

    def index_keys(j, buf, diagonal):
        for rows in chunks:
            sc = None
            for h in range(IDX_HEADS):
                term = jnp.maximum(buf[h, rows, :], 0.0) * iwt_ref[h:h + 1, :]
                sc = term if sc is None else sc + term
            if diagonal:
                sc = jnp.where(admissible(rows), sc, NEG_INF)
            bits = pltpu.bitcast(sc, I32)
            sign = lax.shift_right_arithmetic(bits, 31)
            key_ref[j, rows, :] = (bits ^ (sign & 0x7FFFFFFF)) - sign
        for lanes in (slice(0, 128), slice(128, 256)):
            x = [key_ref[j, SUBLANES * i:SUBLANES * (i + 1), lanes] for i in range(32)]
            step, keep = 16, 0x0000FFFF
            while step:
                k = 0
                while k < 32:
                    t = (x[k] ^ lax.shift_right_logical(x[k + step], step)) & keep
                    x[k] = x[k] ^ t
                    x[k + step] = x[k + step] ^ lax.shift_left(t, step)
                    k = (k + step + 1) & ~step
                step >>= 1
                keep = (keep ^ (keep << step)) & 0xFFFFFFFF
            dst = pl.ds(pl.multiple_of(j * SUBLANES, SUBLANES), SUBLANES)
            for b in range(32):
                plane = x[31 - b]
                planes_ref[b, dst, lanes] = ~plane if b == 31 else plane

    index_logits(0, lg_ref)

    def far_score_pair(i, carry):
        j = 2 * i
        index_logits(j + 1, lgb_ref)
        index_keys(j, lg_ref, False)
        index_logits(j + 2, lg_ref)
        index_keys(j + 1, lgb_ref, False)
        return carry

    lax.fori_loop(0, n // 2, far_score_pair, 0)

    @pl.when(n % 2 == 1)
    def _():
        index_logits(n, lgb_ref)
        index_keys(n - 1, lg_ref, False)
        index_keys(n, lgb_ref, True)

    @pl.when(n % 2 == 0)
    def _():
        index_keys(n, lg_ref, True)

    group_tiles = math.gcd(GROUP_TILES, nt)
    n_groups = n // group_tiles + 1
    group_rows = group_tiles * SUBLANES

    def clear_tail(t, carry):
        dst = pl.ds(pl.multiple_of(t * SUBLANES, SUBLANES), SUBLANES)
        planes_ref[:, dst, :] = jnp.zeros((32, SUBLANES, TILE), I32)
        live_ref[dst, :] = jnp.zeros((SUBLANES, TILE), I32)
        return carry

    lax.fori_loop(n + 1, n_groups * group_tiles, clear_tail, 0)

    def mark_live(t, carry):
        live_ref[pl.ds(pl.multiple_of(t * SUBLANES, SUBLANES), SUBLANES), :] = jnp.full((SUBLANES, TILE), -1, I32)
        return carry

    lax.fori_loop(0, n + 1, mark_live, 0)

    def group(g):
        return pl.ds(pl.multiple_of(g * group_rows, group_rows), group_rows)

    def count_live(b):
        def body(g, c):
            ones = live_ref[group(g), :]
            if b is not None:
                ones = ones & planes_ref[b, group(g), :]
            pc = lax.population_count(ones)
            for r in range(group_tiles):
                c = c + pc[r * SUBLANES:(r + 1) * SUBLANES, :]
            return c
        c = lax.fori_loop(0, n_groups, body, jnp.zeros((SUBLANES, TILE), I32))
        return jnp.sum(c, axis=0, keepdims=True)

    def bit_step(t, carry):
        need, thr_u, total = carry
        b = 31 - t
        take = jnp.where(total >= need, -1, 0)
        need = jnp.where(total >= need, need, need - total)
        thr_u = thr_u | (take & lax.shift_left(jnp.int32(1), b))
        b_next = jnp.maximum(b - 1, 0)

        def narrow_and_count(g, c):
            live = live_ref[group(g), :]
            ones = live & planes_ref[b, group(g), :]
            live = (live ^ ones) ^ (live & take)
            live_ref[group(g), :] = live
            pc = lax.population_count(live & planes_ref[b_next, group(g), :])
            for r in range(group_tiles):
                c = c + pc[r * SUBLANES:(r + 1) * SUBLANES, :]
            return c

        c = lax.fori_loop(0, n_groups, narrow_and_count, jnp.zeros((SUBLANES, TILE), I32))
        return need, thr_u, jnp.sum(c, axis=0, keepdims=True)

    quota, thr_u, _ = lax.fori_loop(0, 32, bit_step,
                                    (jnp.full((1, TILE), TOPK, I32), jnp.zeros((1, TILE), I32), count_live(31)))
    thr = thr_u ^ SIGN_BIT
    n_equal = count_live(None)
    ties_cut = jnp.max(n_equal - quota) > 0

    am_ref[nt + 2 * N_HEADS] = jnp.full((TILE, TILE), NEG_INF, F32)

    def write_masks(with_ties):
        quota_f = quota.astype(F32)

        def selected(j, rows, taken, diagonal):
            k = key_ref[j, rows, :]
            open_ = jnp.where(admissible(rows), 0.0, NEG_INF) if diagonal else 0.0
            if not with_ties:
                return jnp.where(k >= thr, open_, NEG_INF), taken
            eq = jnp.where(k == thr, 1.0, 0.0).astype(BF16)
            before = jnp.dot(tri_ref[rows.start:rows.stop, rows.start:rows.stop], eq,
                             preferred_element_type=F32) + taken
            tied = jnp.where(k == thr, jnp.where(before < quota_f, open_, NEG_INF), NEG_INF)
            return jnp.where(k > thr, open_, tied), taken + jnp.sum(eq.astype(F32), axis=0, keepdims=True)

        def far_mask(j, taken):
            for rows in chunks:
                am, taken = selected(j, rows, taken, False)
                am_ref[j, rows, :] = am
            return taken

        taken = lax.fori_loop(0, n - 1, far_mask, jnp.zeros((1, TILE), F32))

        def near_mask(j, d, taken):
            for rows in chunks:
                am, taken = selected(j, rows, taken, d == 1)
                for h in range(N_HEADS):
                    am_ref[nt + N_HEADS * d + h, rows, :] = am + t5_ref[h, d * TILE + rows.start:d * TILE + rows.stop, :]
            return taken

        prev = near_mask(jnp.maximum(n - 1, 0), 0, taken)
        near_mask(n, 1, jnp.where(n >= 1, prev, taken))

    @pl.when(ties_cut)
    def _():
        write_masks(True)

    @pl.when(jnp.logical_not(ties_cut))
    def _():
        write_masks(False)

    m_ref[...] = jnp.full(m_ref.shape, -1e30, F32)
    acc_ref[...] = jnp.zeros(acc_ref.shape, F32)
    alpha_ref[...] = jnp.ones(alpha_ref.shape, F32)
    pb_ref[...] = jnp.zeros(pb_ref.shape, pb_ref.dtype)
    q = cqk_ref[n, :, 0:GROUP_W]
    lane_head = lax.broadcasted_iota(I32, (TILE, GROUP_W), 1) // HEAD_DIM
    for h in range(N_HEADS):
        qh_ref[h] = jnp.where(lane_head == h, q, jnp.zeros_like(q))

    def masked_logits(j, s_buf):
        k = cqk_ref[jnp.minimum(j, n), :, GROUP_W:2 * GROUP_W]
        near = j - (n - 1)
        for h in range(N_HEADS):
            slab = jnp.where(j > n, nt + 2 * N_HEADS, jnp.where(near >= 0, nt + N_HEADS * near + h, j))
            s = _nt_dot(k, qh_ref[h]) + am_ref[slab]
            s_buf[h] = s
            mx_ref[j % 2, h] = jnp.max(s, axis=0, keepdims=True)

    def softmax_step(j, s_buf, p_buf):
        for h in range(N_HEADS):
            m_old = m_ref[h]
            m_new = jnp.maximum(m_old, mx_ref[j % 2, h])
            for rows in chunks:
                p_buf[h, rows, :] = jnp.exp2(s_buf[h, rows, :] - m_new).astype(p_buf.dtype)
            alpha_ref[h] = jnp.exp2(m_old - m_new)
            m_ref[h] = m_new

    def accumulate(j, p_buf):
        jc = jnp.clip(j, 0, n)
        for h in range(N_HEADS):
            vt = cvt_ref[jc, h * V_ROWS:(h + 1) * V_ROWS, :]
            acc_ref[h] = alpha_ref[h] * acc_ref[h] + jnp.dot(vt, p_buf[h], preferred_element_type=F32)

    masked_logits(0, sa_ref)

    def tile_pair(i, carry):
        j = 2 * i
        accumulate(j - 1, pb_ref)
        masked_logits(j + 1, sb_ref)
        softmax_step(j, sa_ref, pa_ref)
        masked_logits(j + 2, sa_ref)
        accumulate(j, pa_ref)
        softmax_step(j + 1, sb_ref, pb_ref)
        return carry

    lax.fori_loop(0, n_pairs, tile_pair, 0)
    accumulate(2 * n_pairs - 1, pb_ref)

    out_t = jnp.concatenate([acc_ref[h, 0:HEAD_DIM, :] * (1.0 / acc_ref[h, HEAD_DIM:HEAD_DIM + 1, :])
                             for h in range(N_HEADS)], axis=0)
    o_ref[...] = (jnp.transpose(out_t) * gate_ref[...]).astype(o_ref.dtype)


def _dsa_attention(cqk4, cvt, ik4, iq, iw, gates, t5_rel, tri, bsz, seq):
    nt = seq // TILE
    assert nt % 2 == 0
    sdt = cqk4.dtype
    return pl.pallas_call(
        _dsa_kernel,
        grid=(bsz, nt),
        in_specs=[_resident((None, nt, TILE, 512), lambda b, i: (b, 0, 0, 0)),
                  _resident((None, nt, N_HEADS * V_ROWS, TILE), lambda b, i: (b, 0, 0, 0)),
                  _resident((None, nt, TILE, 256), lambda b, i: (b, 0, 0, 0)),
                  pl.BlockSpec((TILE, 512), lambda b, i: (b * nt + i, 0)),
                  pl.BlockSpec((TILE, 128), lambda b, i: (b * nt + i, 0)),
                  pl.BlockSpec((TILE, GROUP_W), lambda b, i: (b * nt + i, 2)),
                  _resident((N_HEADS, 2 * TILE, TILE), lambda b, i: (0, 0, 0)),
                  _resident((TILE, TILE), lambda b, i: (0, 0))],
        out_specs=pl.BlockSpec((TILE, GROUP_W), lambda b, i: (b * nt + i, 0)),
        out_shape=jax.ShapeDtypeStruct((bsz * seq, GROUP_W), sdt),
        scratch_shapes=[pltpu.VMEM((nt, TILE, TILE), I32),
                        pltpu.VMEM((32, nt * SUBLANES, TILE), I32),
                        pltpu.VMEM((nt * SUBLANES, TILE), I32),
                        pltpu.VMEM((nt + 2 * N_HEADS + 1, TILE, TILE), F32),
                        pltpu.VMEM((IDX_HEADS, TILE), F32),
                        pltpu.VMEM((N_HEADS, TILE, GROUP_W), sdt),
                        pltpu.VMEM((IDX_HEADS, TILE, TILE), F32),
                        pltpu.VMEM((IDX_HEADS, TILE, TILE), F32),
                        pltpu.VMEM((N_HEADS, TILE, TILE), F32),
                        pltpu.VMEM((N_HEADS, TILE, TILE), F32),
                        pltpu.VMEM((N_HEADS, TILE, TILE), sdt),
                        pltpu.VMEM((N_HEADS, TILE, TILE), sdt),
                        pltpu.VMEM((N_HEADS, 1, TILE), F32),
                        pltpu.VMEM((2, N_HEADS, 1, TILE), F32),
                        pltpu.VMEM((N_HEADS, 1, TILE), F32),
                        pltpu.VMEM((N_HEADS, V_ROWS, TILE), F32)],
        compiler_params=_cparams(("parallel", "arbitrary")),
        name="dsa_attention",
    )(cqk4, cvt, ik4, iq, iw, gates, t5_rel, tri)


def _outproj_kernel(h_ref, ya_ref, yb_ref, yc_ref, ym_ref, w_ref, b_ref, g_ref, beta_ref, o_ref):
    y = b_ref[...]
    for idx, part in enumerate((ya_ref, yb_ref, yc_ref, ym_ref)):
        y = y + jnp.dot(part[...], w_ref[idx * GROUP_W:(idx + 1) * GROUP_W, :],
                        preferred_element_type=F32)
    o_ref[...] = _layer_norm(DEEPNORM_ALPHA * h_ref[...] + y, g_ref[...], beta_ref[...])


def _out_projection(h, ya, yb, yc, ym, w_out, b_out, g, beta):
    m = h.shape[0]
    part = pl.BlockSpec((ROW_TILE, GROUP_W), lambda i: (i, 0))
    vec = pl.BlockSpec((1, D_MODEL), lambda i: (0, 0))
    return pl.pallas_call(
        _outproj_kernel,
        grid=(m // ROW_TILE,),
        in_specs=[pl.BlockSpec((ROW_TILE, D_MODEL), lambda i: (i, 0)), part, part, part, part,
                  pl.BlockSpec((D_MODEL, D_MODEL), lambda i: (0, 0)), vec, vec, vec],
        out_specs=pl.BlockSpec((ROW_TILE, D_MODEL), lambda i: (i, 0)),
        out_shape=jax.ShapeDtypeStruct((m, D_MODEL), F32),
        compiler_params=_cparams(("parallel",)),
        name="out_projection",
    )(h, ya, yb, yc, ym, w_out, b_out, g, beta)


def _prep_in_weights(w_in, b_in, sdt):
    sp = np.cumsum((0, 256, 256, 256, 256, 256, 256, 256, 256, 256, 256, 256, 512, 64, 8, 256, 256))
    names = ("a_u", "a_v", "a_g", "bq", "bk", "bv", "bg", "cq", "ck", "cv", "cg", "iq", "ik", "iw", "mq", "mg")
    col = {nm: slice(int(sp[i]), int(sp[i + 1])) for i, nm in enumerate(names)}
    qs = HEAD_DIM ** -0.5

    def build(t):
        z = lambda width: jnp.zeros((t.shape[0], width), t.dtype)
        ik = t[:, col["ik"]]
        main = [t[:, col["a_u"]], t[:, col["a_v"]],
                t[:, col["bq"]] * (qs * LOG2E), t[:, col["bk"]], t[:, col["bv"]],
                t[:, col["cq"]] * (qs * LOG2E), t[:, col["ck"]],
                t[:, col["mq"]] * (qs * LOG2E),
                t[:, col["a_g"]], t[:, col["bg"]], t[:, col["cg"]], t[:, col["mg"]]]
        idx = [t[:, col["iq"]], ik, z(64), z(64), ik, t[:, col["iw"]], z(120)]
        return jnp.concatenate(main, axis=1), jnp.concatenate(idx, axis=1), t[:, col["cv"]]

    w_main, w_idx, w_cv = build(w_in)
    b_main, b_idx, b_cv = build(b_in.reshape(1, -1))
    return (w_main.astype(sdt), b_main, w_idx, b_idx, jnp.transpose(w_cv).astype(sdt), jnp.transpose(b_cv))


def _toeplitz(g, n_rows, n_cols):
    span = n_rows + n_cols - 1
    lead = g.shape[:-1]
    period = jnp.concatenate([g, jnp.zeros(lead + (1,), g.dtype)], axis=-1)
    flat = jnp.tile(period, (1,) * len(lead) + (n_rows,))[..., :n_rows * span]
    return flat.reshape(lead + (n_rows, span))[..., n_rows - 1:n_rows - 1 + n_cols]


def _band_bias_table(b_rel):
    m = np.arange(4 * TILE - 1)
    rel_idx = np.clip(3 * TILE - 1 - m, -REL_CLIP, REL_CLIP) + REL_CLIP
    bias = _toeplitz(b_rel[:, jnp.asarray(rel_idx)] * LOG2E, TILE, 3 * TILE)
    i = np.arange(TILE)[:, None]
    j = np.arange(3 * TILE)[None, :]
    chunk_diff = i // CHUNK - j // CHUNK + B_PREV_CHUNKS
    allowed = (chunk_diff >= 0) & (chunk_diff <= B_PREV_CHUNKS)
    bias = jnp.where(jnp.asarray(allowed)[None], bias, NEG_INF).astype(F32)
    slabs = [bias[:, None, :, t * TILE:(t + 1) * TILE] for t in range(3)]
    return jnp.concatenate(slabs + [jnp.full((N_HEADS, 1, TILE, TILE), NEG_INF, F32)], axis=1)


def _t5_bucket(rel):
    nb = T5_BUCKETS // 2
    max_exact = nb // 2
    ret = jnp.where(rel > 0, nb, 0)
    n = jnp.abs(rel)
    nf = jnp.maximum(n, 1).astype(F32)
    large = max_exact + (jnp.log(nf / max_exact) / math.log(T5_MAX_DIST / max_exact)
                         * (nb - max_exact)).astype(I32)
    large = jnp.minimum(large, nb - 1)
    return ret + jnp.where(n < max_exact, n, large)


def _t5_rel_table(t5_table):
    m = jnp.arange(3 * TILE - 1, dtype=I32)
    near = t5_table[_t5_bucket(TILE - 1 - m)]
    far = t5_table[_t5_bucket(jnp.full((1,), -(2 * TILE + 1), I32))]
    return _toeplitz(jnp.transpose(near - far) * LOG2E, 2 * TILE, TILE).astype(F32)


def kernel(x, mem, ln_in_g, ln_in_b, w_in, b_in, a_ln_g, a_ln_b, a_ws, a_bs, b_rel, t5_table,
           w_mem_kv, w_out, b_out, ln_g, ln_b):
    bsz, seq, _ = x.shape
    assert seq % MIX_TILE == 0 and x.shape[2] == D_MODEL
    m = bsz * seq
    nt = seq // TILE

    cpos = np.arange(A_BLOCK) // CHUNK
    a_mask = jnp.asarray(cpos[None, :] <= cpos[:, None])
    t5_rel = _t5_rel_table(t5_table)
    tri = jnp.asarray(np.tril(np.ones((TILE, TILE), np.float32), -1)).astype(BF16)

    h = x.reshape(m, D_MODEL)
    for l in range(DEPTH):
        sdt = BF16 if l == DEPTH - 1 else F32
        w_main, b_main, w_idx, b_idx, wvt, bvt = _prep_in_weights(w_in[l], b_in[l], sdt)
        proj = list(_in_projection(h, w_main, b_main, w_idx, b_idx, wvt, bvt, bsz, seq,
                                   input_norm=(ln_in_g, ln_in_b) if l == 0 else None))
        if l == 0:
            h = proj.pop()
        uv, bqkv, cqk, cvt, iq, ik, iw, mq, gates = proj

        ws_masked = jnp.where(a_mask[None], a_ws[l], 0.0).astype(sdt)
        bs_lanes = jnp.repeat(jnp.transpose(a_bs[l]), HEAD_DIM, axis=1)
        ya = _gmlp(uv, gates, a_ln_g[l].reshape(1, -1), a_ln_b[l].reshape(1, -1), ws_masked, bs_lanes)

        yb = _band_attention(bqkv.reshape(bsz, nt, TILE, 768), gates, _band_bias_table(b_rel[l]), bsz, seq)

        yc = _dsa_attention(cqk.reshape(bsz, nt, TILE, 512), cvt, ik.reshape(bsz, nt, TILE, 256),
                            iq, iw, gates, t5_rel, tri, bsz, seq)

        km, vm = _memory_kv(mem, w_mem_kv[l].astype(sdt))
        ym = _memory_attention(mq, km, vm, gates, bsz, seq)

        h = _out_projection(h, ya, yb, yc, ym, w_out[l].astype(sdt), b_out[l].reshape(1, -1),
                            ln_g[l].reshape(1, -1), ln_b[l].reshape(1, -1))
    return h.reshape(bsz, seq, D_MODEL)
```

```python
import functools
import math

import numpy as np
import jax
import jax.numpy as jnp
from jax import lax
from jax.experimental import pallas as pl
from jax.experimental.pallas import tpu as pltpu

F32 = jnp.float32
BF16 = jnp.bfloat16
I32 = jnp.int32

D_MODEL = 1024
CHUNK = 64
HEAD_DIM = 64
GROUP_W = 256
N_HEADS = GROUP_W // HEAD_DIM
A_BLOCK = 128
B_PREV_CHUNKS = 8
REL_CLIP = 128
IDX_HEADS = 8
IDX_DIM = 64
TOPK = 256
T5_BUCKETS = 32
T5_MAX_DIST = 128
N_MEM = 256
DEPTH = 2
DEEPNORM_ALPHA = (2 * DEPTH) ** 0.25
LN_EPS = 1e-5

TILE = 256
ROW_TILE = 512
MIX_TILE = 2048
VMEM_LIMIT = 56 * 1024 * 1024
NEG_INF = float("-inf")
SIGN_BIT = -2 ** 31
SUBLANES = 8
GROUP_TILES = 8
V_ROWS = HEAD_DIM + 16
ROW_CHUNK = 64
LOG2E = 1.4426950408889634

C_UV = (0, 512)
C_BQKV = (512, 1280)
C_CQK = (1280, 1792)
C_MQ = (1792, 2048)
C_GATE = (2048, 3072)
N_MAIN = 3072
C_IQ = (0, 512)
C_IK = (512, 768)
C_IW = (768, 896)
N_IDX = 896


def _cparams(sem):
    return pltpu.CompilerParams(dimension_semantics=sem, vmem_limit_bytes=VMEM_LIMIT)


def _resident(block_shape, index_map):
    return pl.BlockSpec(block_shape, index_map, pipeline_mode=pl.Buffered(1))


def _nt_dot(a, b):
    return lax.dot_general(a, b, (((1,), (1,)), ((), ())), preferred_element_type=F32)


def _gelu_tanh(x):
    return 0.5 * x * (1.0 + jnp.tanh(math.sqrt(2.0 / math.pi) * (x + 0.044715 * (x * x * x))))


def _silu(x):
    return x * (1.0 / (1.0 + jnp.exp(-x)))


def _layer_norm(x, g, b):
    mu = jnp.mean(x, axis=-1, keepdims=True)
    xc = x - mu
    var = jnp.mean(xc * xc, axis=-1, keepdims=True)
    return xc * lax.rsqrt(var + LN_EPS) * g + b


def _inproj_kernel(*refs, normalise_input):
    if normalise_input:
        h_ref, lng_ref, lnb_ref, *refs = refs
        *refs, hout_ref = refs
    else:
        h_ref, *refs = refs
    (w_ref, b_ref, widx_ref, bidx_ref, wvt_ref, bvt_ref,
     uv_ref, bqkv_ref, cqk_ref, cvt_ref, iq_ref, ik_ref, iw_ref, mq_ref, gate_ref) = refs
    x = h_ref[...]
    if normalise_input:
        x = _layer_norm(x, lng_ref[...], lnb_ref[...])
        hout_ref[...] = x
    xm = x.astype(w_ref.dtype)

    def seg(c):
        return jnp.dot(xm, w_ref[:, c[0]:c[1]], preferred_element_type=F32) + b_ref[:, c[0]:c[1]]

    def idx_seg(c):
        return jnp.dot(x, widx_ref[:, c[0]:c[1]], preferred_element_type=F32) + bidx_ref[:, c[0]:c[1]]

    uv_ref[...] = _gelu_tanh(seg(C_UV))
    bqkv_ref[...] = seg(C_BQKV).astype(bqkv_ref.dtype)
    cqk_ref[...] = seg(C_CQK).astype(cqk_ref.dtype)
    mq_ref[...] = seg(C_MQ).astype(mq_ref.dtype)
    gate_ref[...] = _silu(seg(C_GATE))
    iq_ref[...] = idx_seg(C_IQ)
    ik_ref[...] = idx_seg(C_IK)
    iw_ref[...] = idx_seg(C_IW)
    for t in range(ROW_TILE // TILE):
        vt = (_nt_dot(wvt_ref[...], xm[t * TILE:(t + 1) * TILE, :]) + bvt_ref[...]).astype(cvt_ref.dtype)
        for h in range(N_HEADS):
            cvt_ref[t, h * V_ROWS:h * V_ROWS + HEAD_DIM, :] = vt[h * HEAD_DIM:(h + 1) * HEAD_DIM, :]
            cvt_ref[t, h * V_ROWS + HEAD_DIM:(h + 1) * V_ROWS, :] = jnp.ones((V_ROWS - HEAD_DIM, TILE), cvt_ref.dtype)


def _in_projection(h, w_main, b_main, w_idx, b_idx, wvt, bvt, bsz, seq, input_norm=None):
    m = h.shape[0]
    sdt = w_main.dtype
    steps_per_batch = seq // ROW_TILE
    tiles_per_step = ROW_TILE // TILE

    def rows(width, dtype):
        return (pl.BlockSpec((ROW_TILE, width), lambda i: (i, 0)),
                jax.ShapeDtypeStruct((m, width), dtype))

    outs = [rows(512, F32), rows(768, sdt), rows(512, sdt),
            (pl.BlockSpec((None, tiles_per_step, N_HEADS * V_ROWS, TILE),
                          lambda i: (i // steps_per_batch, i % steps_per_batch, 0, 0)),
             jax.ShapeDtypeStruct((bsz, seq // TILE, N_HEADS * V_ROWS, TILE), sdt)),
            rows(512, F32), rows(256, F32), rows(128, F32), rows(256, sdt), rows(1024, F32)]
    vec = pl.BlockSpec((1, D_MODEL), lambda i: (0, 0))
    norm_specs, norm_args = [], []
    if input_norm is not None:
        outs.append(rows(D_MODEL, F32))
        norm_specs = [vec, vec]
        norm_args = [p.reshape(1, D_MODEL) for p in input_norm]
    return pl.pallas_call(
        functools.partial(_inproj_kernel, normalise_input=input_norm is not None),
        grid=(m // ROW_TILE,),
        in_specs=[pl.BlockSpec((ROW_TILE, D_MODEL), lambda i: (i, 0)), *norm_specs,
                  _resident((D_MODEL, N_MAIN), lambda i: (0, 0)),
                  pl.BlockSpec((1, N_MAIN), lambda i: (0, 0)),
                  _resident((D_MODEL, N_IDX), lambda i: (0, 0)),
                  pl.BlockSpec((1, N_IDX), lambda i: (0, 0)),
                  _resident((GROUP_W, D_MODEL), lambda i: (0, 0)),
                  pl.BlockSpec((GROUP_W, 1), lambda i: (0, 0))],
        out_specs=[o[0] for o in outs],
        out_shape=[o[1] for o in outs],
        compiler_params=_cparams(("parallel",)),
        name="in_projection",
    )(h, *norm_args, w_main, b_main, w_idx, b_idx, wvt, bvt)


def _gmlp_kernel(uv_ref, gate_ref, lng_ref, lnb_ref, ws_ref, bs_ref, o_ref):
    lane_group = lax.broadcasted_iota(I32, (A_BLOCK, GROUP_W), 1) // HEAD_DIM
    for blk in range(MIX_TILE // A_BLOCK):
        r = slice(blk * A_BLOCK, (blk + 1) * A_BLOCK)
        u = uv_ref[r, 0:GROUP_W]
        v = _layer_norm(uv_ref[r, GROUP_W:2 * GROUP_W], lng_ref[...], lnb_ref[...])
        vb = v.astype(ws_ref.dtype)
        mixed = bs_ref[...]
        for g in range(N_HEADS):
            mg = jnp.dot(ws_ref[g], vb, preferred_element_type=F32)
            mixed = mixed + jnp.where(lane_group == g, mg, 0.0)
        o_ref[r, :] = ((u * mixed) * gate_ref[r, :]).astype(o_ref.dtype)


def _gmlp(uv, gates, lng, lnb, ws_masked, bs_lanes):
    m = uv.shape[0]
    return pl.pallas_call(
        _gmlp_kernel,
        grid=(m // MIX_TILE,),
        in_specs=[pl.BlockSpec((MIX_TILE, 512), lambda i: (i, 0)),
                  pl.BlockSpec((MIX_TILE, GROUP_W), lambda i: (i, 0)),
                  pl.BlockSpec((1, GROUP_W), lambda i: (0, 0)),
                  pl.BlockSpec((1, GROUP_W), lambda i: (0, 0)),
                  pl.BlockSpec((N_HEADS, A_BLOCK, A_BLOCK), lambda i: (0, 0, 0)),
                  pl.BlockSpec((A_BLOCK, GROUP_W), lambda i: (0, 0))],
        out_specs=pl.BlockSpec((MIX_TILE, GROUP_W), lambda i: (i, 0)),
        out_shape=jax.ShapeDtypeStruct((m, GROUP_W), ws_masked.dtype),
        compiler_params=_cparams(("parallel",)),
        name="gmlp",
    )(uv, gates, lng, lnb, ws_masked, bs_lanes)


def _band_kernel(qkv_ref, gate_ref, bias_ref, o_ref):
    i = pl.program_id(1)
    q = qkv_ref[i, :, 0:GROUP_W]
    lane_head = lax.broadcasted_iota(I32, (TILE, GROUP_W), 1) // HEAD_DIM
    ks, vs, slabs = [], [], []
    for t, d in enumerate((2, 1, 0)):
        kb = i - d
        kbc = jnp.maximum(kb, 0)
        ks.append(qkv_ref[kbc, :, GROUP_W:2 * GROUP_W])
        vs.append(qkv_ref[kbc, :, 2 * GROUP_W:3 * GROUP_W])
        slabs.append(jnp.where(kb >= 0, t, 3))
    out = jnp.zeros((TILE, GROUP_W), F32)
    for h in range(N_HEADS):
        qh = jnp.where(lane_head == h, q, jnp.zeros_like(q))
        ss = [_nt_dot(qh, ks[t]) + bias_ref[h, slabs[t]] for t in range(3)]
        mx = jnp.maximum(jnp.maximum(jnp.max(ss[0], axis=1, keepdims=True),
                                     jnp.max(ss[1], axis=1, keepdims=True)),
                         jnp.max(ss[2], axis=1, keepdims=True))
        den = jnp.zeros((TILE, 1), F32)
        pv = jnp.zeros((TILE, GROUP_W), F32)
        for t in range(3):
            p = jnp.exp2(ss[t] - mx)
            den = den + jnp.sum(p, axis=1, keepdims=True)
            pv = pv + jnp.dot(p.astype(vs[t].dtype), vs[t], preferred_element_type=F32)
        out = jnp.where(lane_head == h, pv * (1.0 / den), out)
    o_ref[...] = (out * gate_ref[...]).astype(o_ref.dtype)


def _band_attention(bqkv4, gates, bias, bsz, seq):
    nt = seq // TILE
    return pl.pallas_call(
        _band_kernel,
        grid=(bsz, nt),
        in_specs=[pl.BlockSpec((None, nt, TILE, 768), lambda b, i: (b, 0, 0, 0)),
                  pl.BlockSpec((TILE, GROUP_W), lambda b, i: (b * nt + i, 1)),
                  _resident((N_HEADS, 4, TILE, TILE), lambda b, i: (0, 0, 0, 0))],
        out_specs=pl.BlockSpec((TILE, GROUP_W), lambda b, i: (b * nt + i, 0)),
        out_shape=jax.ShapeDtypeStruct((bsz * seq, GROUP_W), bqkv4.dtype),
        compiler_params=_cparams(("parallel", "arbitrary")),
        name="band_attention",
    )(bqkv4, gates, bias)


def _memkv_kernel(mem_ref, w_ref, km_ref, vm_ref):
    kv = jnp.dot(mem_ref[...].astype(w_ref.dtype), w_ref[...], preferred_element_type=F32)
    km_ref[...] = kv[:, 0:GROUP_W].astype(km_ref.dtype)
    vm_ref[...] = kv[:, GROUP_W:2 * GROUP_W].astype(vm_ref.dtype)


def _memory_kv(mem, w_kv):
    bsz, n_mem, _ = mem.shape
    blk = pl.BlockSpec((None, n_mem, GROUP_W), lambda b: (b, 0, 0))
    shp = jax.ShapeDtypeStruct((bsz, n_mem, GROUP_W), w_kv.dtype)
    return pl.pallas_call(
        _memkv_kernel,
        grid=(bsz,),
        in_specs=[pl.BlockSpec((None, n_mem, D_MODEL), lambda b: (b, 0, 0)),
                  pl.BlockSpec((D_MODEL, 2 * GROUP_W), lambda b: (0, 0))],
        out_specs=[blk, blk],
        out_shape=[shp, shp],
        compiler_params=_cparams(("parallel",)),
        name="memory_kv",
    )(mem, w_kv)


def _memattn_kernel(q_ref, km_ref, vm_ref, gate_ref, o_ref):
    q = q_ref[...]
    lane_head = lax.broadcasted_iota(I32, (MIX_TILE, GROUP_W), 1) // HEAD_DIM
    out = jnp.zeros((MIX_TILE, GROUP_W), F32)
    for h in range(N_HEADS):
        qh = jnp.where(lane_head == h, q, jnp.zeros_like(q))
        s = _nt_dot(qh, km_ref[...])
        p = jnp.exp2(s - jnp.max(s, axis=1, keepdims=True))
        den = jnp.sum(p, axis=1, keepdims=True)
        pv = jnp.dot(p.astype(vm_ref.dtype), vm_ref[...], preferred_element_type=F32)
        out = jnp.where(lane_head == h, pv * (1.0 / den), out)
    o_ref[...] = (out * gate_ref[...]).astype(o_ref.dtype)


def _memory_attention(mq, km, vm, gates, bsz, seq):
    steps = seq // MIX_TILE
    n_mem = km.shape[1]
    return pl.pallas_call(
        _memattn_kernel,
        grid=(bsz, steps),
        in_specs=[pl.BlockSpec((MIX_TILE, GROUP_W), lambda b, i: (b * steps + i, 0)),
                  pl.BlockSpec((None, n_mem, GROUP_W), lambda b, i: (b, 0, 0)),
                  pl.BlockSpec((None, n_mem, GROUP_W), lambda b, i: (b, 0, 0)),
                  pl.BlockSpec((MIX_TILE, GROUP_W), lambda b, i: (b * steps + i, 3))],
        out_specs=pl.BlockSpec((MIX_TILE, GROUP_W), lambda b, i: (b * steps + i, 0)),
        out_shape=jax.ShapeDtypeStruct((bsz * seq, GROUP_W), mq.dtype),
        compiler_params=_cparams(("parallel", "arbitrary")),
        name="memory_attention",
    )(mq, km, vm, gates)


def _dsa_kernel(cqk_ref, cvt_ref, ik_ref, iq_ref, iw_ref, gate_ref, t5_ref, tri_ref, o_ref,
                key_ref, planes_ref, live_ref, am_ref, iwt_ref, qh_ref, lg_ref, lgb_ref, sa_ref, sb_ref,
                pa_ref, pb_ref, m_ref, mx_ref, alpha_ref, acc_ref):
    n = pl.program_id(1)
    nt = key_ref.shape[0]
    q0 = n * TILE
    n_pairs = n // 2 + 1
    score_scale = (IDX_DIM ** -0.5) * (IDX_HEADS ** -0.5)
    chunks = [slice(c * ROW_CHUNK, (c + 1) * ROW_CHUNK) for c in range(TILE // ROW_CHUNK)]

    q_limit = ((q0 + lax.broadcasted_iota(I32, (1, TILE), 1)) // CHUNK + 1) * CHUNK
    chunk_pos = lax.broadcasted_iota(I32, (ROW_CHUNK, 1), 0)

    def admissible(rows):
        return q0 + rows.start + chunk_pos < q_limit

    iwt_ref[...] = jnp.transpose(iw_ref[...])[0:IDX_HEADS, :] * score_scale

    def index_logits(j, buf):
        for p in range(IDX_HEADS // 2):
            for parity in range(2):
                buf[2 * p + parity] = _nt_dot(ik_ref[j, :, parity * 128:(parity + 1) * 128],
                                              iq_ref[:, p * 128:(p + 1) * 128])

    def index_keys(j, buf, diagonal):
        for rows in chunks:
            sc = None
            for h in range(IDX_HEADS):
                term = jnp.maximum(buf[h, rows, :], 0.0) * iwt_ref[h:h + 1, :]
                sc = term if sc is None else sc + term
            if diagonal:
                sc = jnp.where(admissible(rows), sc, NEG_INF)
            bits = pltpu.bitcast(sc, I32)
            sign = lax.shift_right_arithmetic(bits, 31)
            key_ref[j, rows, :] = (bits ^ (sign & 0x7FFFFFFF)) - sign
        for lanes in (slice(0, 128), slice(128, 256)):
            x = [key_ref[j, SUBLANES * i:SUBLANES * (i + 1), lanes] for i in range(32)]
            step, keep = 16, 0x0000FFFF
            while step:
                k = 0
                while k < 32:
                    t = (x[k] ^ lax.shift_right_logical(x[k + step], step)) & keep
                    x[k] = x[k] ^ t
                    x[k + step] = x[k + step] ^ lax.shift_left(t, step)
                    k = (k + step + 1) & ~step
                step >>= 1
                keep = (keep ^ (keep << step)) & 0xFFFFFFFF
            dst = pl.ds(pl.multiple_of(j * SUBLANES, SUBLANES), SUBLANES)
            for b in range(32):
                plane = x[31 - b]
                planes_ref[b, dst, lanes] = ~plane if b == 31 else plane

    index_logits(0, lg_ref)

    def far_score_pair(i, carry):
        j = 2 * i
        index_logits(j + 1, lgb_ref)
        index_keys(j, lg_ref, False)
        index_logits(j + 2, lg_ref)
        index_keys(j + 1, lgb_ref, False)
        return carry

    lax.fori_loop(0, n // 2, far_score_pair, 0)

    @pl.when(n % 2 == 1)
    def _():
        index_logits(n, lgb_ref)
        index_keys(n - 1, lg_ref, False)
        index_keys(n, lgb_ref, True)

    @pl.when(n % 2 == 0)
    def _():
        index_keys(n, lg_ref, True)

    group_tiles = math.gcd(GROUP_TILES, nt)
    n_groups = n // group_tiles + 1
    group_rows = group_tiles * SUBLANES

    def clear_tail(t, carry):
        dst = pl.ds(pl.multiple_of(t * SUBLANES, SUBLANES), SUBLANES)
        planes_ref[:, dst, :] = jnp.zeros((32, SUBLANES, TILE), I32)
        live_ref[dst, :] = jnp.zeros((SUBLANES, TILE), I32)
        return carry

    lax.fori_loop(n + 1, n_groups * group_tiles, clear_tail, 0)

    def mark_live(t, carry):
        live_ref[pl.ds(pl.multiple_of(t * SUBLANES, SUBLANES), SUBLANES), :] = jnp.full((SUBLANES, TILE), -1, I32)
        return carry

    lax.fori_loop(0, n + 1, mark_live, 0)

    def group(g):
        return pl.ds(pl.multiple_of(g * group_rows, group_rows), group_rows)

    def count_live(b):
        def body(g, c):
            ones = live_ref[group(g), :]
            if b is not None:
                ones = ones & planes_ref[b, group(g), :]
            pc = lax.population_count(ones)
            for r in range(group_tiles):
                c = c + pc[r * SUBLANES:(r + 1) * SUBLANES, :]
            return c
        c = lax.fori_loop(0, n_groups, body, jnp.zeros((SUBLANES, TILE), I32))
        return jnp.sum(c, axis=0, keepdims=True)

    def bit_step(t, carry):
        need, thr_u, total = carry
        b = 31 - t
        take = jnp.where(total >= need, -1, 0)
        need = jnp.where(total >= need, need, need - total)
        thr_u = thr_u | (take & lax.shift_left(jnp.int32(1), b))
        b_next = jnp.maximum(b - 1, 0)

        def narrow_and_count(g, c):
            live = live_ref[group(g), :]
            ones = live & planes_ref[b, group(g), :]
            live = (live ^ ones) ^ (live & take)
            live_ref[group(g), :] = live
            pc = lax.population_count(live & planes_ref[b_next, group(g), :])
            for r in range(group_tiles):
                c = c + pc[r * SUBLANES:(r + 1) * SUBLANES, :]
            return c

        c = lax.fori_loop(0, n_groups, narrow_and_count, jnp.zeros((SUBLANES, TILE), I32))
        return need, thr_u, jnp.sum(c, axis=0, keepdims=True)

    quota, thr_u, _ = lax.fori_loop(0, 32, bit_step,
                                    (jnp.full((1, TILE), TOPK, I32), jnp.zeros((1, TILE), I32), count_live(31)))
    thr = thr_u ^ SIGN_BIT
    n_equal = count_live(None)
    ties_cut = jnp.max(n_equal - quota) > 0

    am_ref[nt + 2 * N_HEADS] = jnp.full((TILE, TILE), NEG_INF, F32)

    def write_masks(with_ties):
        quota_f = quota.astype(F32)

        def selected(j, rows, taken, diagonal):
            k = key_ref[j, rows, :]
            open_ = jnp.where(admissible(rows), 0.0, NEG_INF) if diagonal else 0.0
            if not with_ties:
                return jnp.where(k >= thr, open_, NEG_INF), taken
            eq = jnp.where(k == thr, 1.0, 0.0).astype(BF16)
            before = jnp.dot(tri_ref[rows.start:rows.stop, rows.start:rows.stop], eq,
                             preferred_element_type=F32) + taken
            tied = jnp.where(k == thr, jnp.where(before < quota_f, open_, NEG_INF), NEG_INF)
            return jnp.where(k > thr, open_, tied), taken + jnp.sum(eq.astype(F32), axis=0, keepdims=True)

        def far_mask(j, taken):
            for rows in chunks:
                am, taken = selected(j, rows, taken, False)
                am_ref[j, rows, :] = am
            return taken

        taken = lax.fori_loop(0, n - 1, far_mask, jnp.zeros((1, TILE), F32))

        def near_mask(j, d, taken):
            for rows in chunks:
                am, taken = selected(j, rows, taken, d == 1)
                for h in range(N_HEADS):
                    am_ref[nt + N_HEADS * d + h, rows, :] = am + t5_ref[h, d * TILE + rows.start:d * TILE + rows.stop, :]
            return taken

        prev = near_mask(jnp.maximum(n - 1, 0), 0, taken)
        near_mask(n, 1, jnp.where(n >= 1, prev, taken))

    @pl.when(ties_cut)
    def _():
        write_masks(True)

    @pl.when(jnp.logical_not(ties_cut))
    def _():
        write_masks(False)

    m_ref[...] = jnp.full(m_ref.shape, -1e30, F32)
    acc_ref[...] = jnp.zeros(acc_ref.shape, F32)
    alpha_ref[...] = jnp.ones(alpha_ref.shape, F32)
    pb_ref[...] = jnp.zeros(pb_ref.shape, pb_ref.dtype)
    q = cqk_ref[n, :, 0:GROUP_W]
    lane_head = lax.broadcasted_iota(I32, (TILE, GROUP_W), 1) // HEAD_DIM
    for h in range(N_HEADS):
        qh_ref[h] = jnp.where(lane_head == h, q, jnp.zeros_like(q))

    def masked_logits(j, s_buf):
        k = cqk_ref[jnp.minimum(j, n), :, GROUP_W:2 * GROUP_W]
        near = j - (n - 1)
        for h in range(N_HEADS):
            slab = jnp.where(j > n, nt + 2 * N_HEADS, jnp.where(near >= 0, nt + N_HEADS * near + h, j))
            s = _nt_dot(k, qh_ref[h]) + am_ref[slab]
            s_buf[h] = s
            mx_ref[j % 2, h] = jnp.max(s, axis=0, keepdims=True)

    def softmax_step(j, s_buf, p_buf):
        for h in range(N_HEADS):
            m_old = m_ref[h]
            m_new = jnp.maximum(m_old, mx_ref[j % 2, h])
            for rows in chunks:
                p_buf[h, rows, :] = jnp.exp2(s_buf[h, rows, :] - m_new).astype(p_buf.dtype)
            alpha_ref[h] = jnp.exp2(m_old - m_new)
            m_ref[h] = m_new

    def accumulate(j, p_buf):
        jc = jnp.clip(j, 0, n)
        for h in range(N_HEADS):
            vt = cvt_ref[jc, h * V_ROWS:(h + 1) * V_ROWS, :]
            acc_ref[h] = alpha_ref[h] * acc_ref[h] + jnp.dot(vt, p_buf[h], preferred_element_type=F32)

    masked_logits(0, sa_ref)

    def tile_pair(i, carry):
        j = 2 * i
        accumulate(j - 1, pb_ref)
        masked_logits(j + 1, sb_ref)
        softmax_step(j, sa_ref, pa_ref)
        masked_logits(j + 2, sa_ref)
        accumulate(j, pa_ref)
        softmax_step(j + 1, sb_ref, pb_ref)
        return carry

    lax.fori_loop(0, n_pairs, tile_pair, 0)
    accumulate(2 * n_pairs - 1, pb_ref)

    out_t = jnp.concatenate([acc_ref[h, 0:HEAD_DIM, :] * (1.0 / acc_ref[h, HEAD_DIM:HEAD_DIM + 1, :])
                             for h in range(N_HEADS)], axis=0)
    o_ref[...] = (jnp.transpose(out_t) * gate_ref[...]).astype(o_ref.dtype)


def _dsa_attention(cqk4, cvt, ik4, iq, iw, gates, t5_rel, tri, bsz, seq):
    nt = seq // TILE
    assert nt % 2 == 0
    sdt = cqk4.dtype
    return pl.pallas_call(
        _dsa_kernel,
        grid=(bsz, nt),
        in_specs=[_resident((None, nt, TILE, 512), lambda b, i: (b, 0, 0, 0)),
                  _resident((None, nt, N_HEADS * V_ROWS, TILE), lambda b, i: (b, 0, 0, 0)),
                  _resident((None, nt, TILE, 256), lambda b, i: (b, 0, 0, 0)),
                  pl.BlockSpec((TILE, 512), lambda b, i: (b * nt + i, 0)),
                  pl.BlockSpec((TILE, 128), lambda b, i: (b * nt + i, 0)),
                  pl.BlockSpec((TILE, GROUP_W), lambda b, i: (b * nt + i, 2)),
                  _resident((N_HEADS, 2 * TILE, TILE), lambda b, i: (0, 0, 0)),
                  _resident((TILE, TILE), lambda b, i: (0, 0))],
        out_specs=pl.BlockSpec((TILE, GROUP_W), lambda b, i: (b * nt + i, 0)),
        out_shape=jax.ShapeDtypeStruct((bsz * seq, GROUP_W), sdt),
        scratch_shapes=[pltpu.VMEM((nt, TILE, TILE), I32),
                        pltpu.VMEM((32, nt * SUBLANES, TILE), I32),
                        pltpu.VMEM((nt * SUBLANES, TILE), I32),
                        pltpu.VMEM((nt + 2 * N_HEADS + 1, TILE, TILE), F32),
                        pltpu.VMEM((IDX_HEADS, TILE), F32),
                        pltpu.VMEM((N_HEADS, TILE, GROUP_W), sdt),
                        pltpu.VMEM((IDX_HEADS, TILE, TILE), F32),
                        pltpu.VMEM((IDX_HEADS, TILE, TILE), F32),
                        pltpu.VMEM((N_HEADS, TILE, TILE), F32),
                        pltpu.VMEM((N_HEADS, TILE, TILE), F32),
                        pltpu.VMEM((N_HEADS, TILE, TILE), sdt),
                        pltpu.VMEM((N_HEADS, TILE, TILE), sdt),
                        pltpu.VMEM((N_HEADS, 1, TILE), F32),
                        pltpu.VMEM((2, N_HEADS, 1, TILE), F32),
                        pltpu.VMEM((N_HEADS, 1, TILE), F32),
                        pltpu.VMEM((N_HEADS, V_ROWS, TILE), F32)],
        compiler_params=_cparams(("parallel", "arbitrary")),
        name="dsa_attention",
    )(cqk4, cvt, ik4, iq, iw, gates, t5_rel, tri)


def _outproj_kernel(h_ref, ya_ref, yb_ref, yc_ref, ym_ref, w_ref, b_ref, g_ref, beta_ref, o_ref):
    y = b_ref[...]
    for idx, part in enumerate((ya_ref, yb_ref, yc_ref, ym_ref)):
        y = y + jnp.dot(part[...], w_ref[idx * GROUP_W:(idx + 1) * GROUP_W, :],
                        preferred_element_type=F32)
    o_ref[...] = _layer_norm(DEEPNORM_ALPHA * h_ref[...] + y, g_ref[...], beta_ref[...])


def _out_projection(h, ya, yb, yc, ym, w_out, b_out, g, beta):
    m = h.shape[0]
    part = pl.BlockSpec((ROW_TILE, GROUP_W), lambda i: (i, 0))
    vec = pl.BlockSpec((1, D_MODEL), lambda i: (0, 0))
    return pl.pallas_call(
        _outproj_kernel,
        grid=(m // ROW_TILE,),
        in_specs=[pl.BlockSpec((ROW_TILE, D_MODEL), lambda i: (i, 0)), part, part, part, part,
                  pl.BlockSpec((D_MODEL, D_MODEL), lambda i: (0, 0)), vec, vec, vec],
        out_specs=pl.BlockSpec((ROW_TILE, D_MODEL), lambda i: (i, 0)),
        out_shape=jax.ShapeDtypeStruct((m, D_MODEL), F32),
        compiler_params=_cparams(("parallel",)),
        name="out_projection",
    )(h, ya, yb, yc, ym, w_out, b_out, g, beta)


def _prep_in_weights(w_in, b_in, sdt):
    sp = np.cumsum((0, 256, 256, 256, 256, 256, 256, 256, 256, 256, 256, 256, 512, 64, 8, 256, 256))
    names = ("a_u", "a_v", "a_g", "bq", "bk", "bv", "bg", "cq", "ck", "cv", "cg", "iq", "ik", "iw", "mq", "mg")
    col = {nm: slice(int(sp[i]), int(sp[i + 1])) for i, nm in enumerate(names)}
    qs = HEAD_DIM ** -0.5

    def build(t):
        z = lambda width: jnp.zeros((t.shape[0], width), t.dtype)
        ik = t[:, col["ik"]]
        main = [t[:, col["a_u"]], t[:, col["a_v"]],
                t[:, col["bq"]] * (qs * LOG2E), t[:, col["bk"]], t[:, col["bv"]],
                t[:, col["cq"]] * (qs * LOG2E), t[:, col["ck"]],
                t[:, col["mq"]] * (qs * LOG2E),
                t[:, col["a_g"]], t[:, col["bg"]], t[:, col["cg"]], t[:, col["mg"]]]
        idx = [t[:, col["iq"]], ik, z(64), z(64), ik, t[:, col["iw"]], z(120)]
        return jnp.concatenate(main, axis=1), jnp.concatenate(idx, axis=1), t[:, col["cv"]]

    w_main, w_idx, w_cv = build(w_in)
    b_main, b_idx, b_cv = build(b_in.reshape(1, -1))
    return (w_main.astype(sdt), b_main, w_idx, b_idx, jnp.transpose(w_cv).astype(sdt), jnp.transpose(b_cv))


def _toeplitz(g, n_rows, n_cols):
    span = n_rows + n_cols - 1
    lead = g.shape[:-1]
    period = jnp.concatenate([g, jnp.zeros(lead + (1,), g.dtype)], axis=-1)
    flat = jnp.tile(period, (1,) * len(lead) + (n_rows,))[..., :n_rows * span]
    return flat.reshape(lead + (n_rows, span))[..., n_rows - 1:n_rows - 1 + n_cols]


def _band_bias_table(b_rel):
    m = np.arange(4 * TILE - 1)
    rel_idx = np.clip(3 * TILE - 1 - m, -REL_CLIP, REL_CLIP) + REL_CLIP
    bias = _toeplitz(b_rel[:, jnp.asarray(rel_idx)] * LOG2E, TILE, 3 * TILE)
    i = np.arange(TILE)[:, None]
    j = np.arange(3 * TILE)[None, :]
    chunk_diff = i // CHUNK - j // CHUNK + B_PREV_CHUNKS
    allowed = (chunk_diff >= 0) & (chunk_diff <= B_PREV_CHUNKS)
    bias = jnp.where(jnp.asarray(allowed)[None], bias, NEG_INF).astype(F32)
    slabs = [bias[:, None, :, t * TILE:(t + 1) * TILE] for t in range(3)]
    return jnp.concatenate(slabs + [jnp.full((N_HEADS, 1, TILE, TILE), NEG_INF, F32)], axis=1)


def _t5_bucket(rel):
    nb = T5_BUCKETS // 2
    max_exact = nb // 2
    ret = jnp.where(rel > 0, nb, 0)
    n = jnp.abs(rel)
    nf = jnp.maximum(n, 1).astype(F32)
    large = max_exact + (jnp.log(nf / max_exact) / math.log(T5_MAX_DIST / max_exact)
                         * (nb - max_exact)).astype(I32)
    large = jnp.minimum(large, nb - 1)
    return ret + jnp.where(n < max_exact, n, large)


def _t5_rel_table(t5_table):
    m = jnp.arange(3 * TILE - 1, dtype=I32)
    near = t5_table[_t5_bucket(TILE - 1 - m)]
    far = t5_table[_t5_bucket(jnp.full((1,), -(2 * TILE + 1), I32))]
    return _toeplitz(jnp.transpose(near - far) * LOG2E, 2 * TILE, TILE).astype(F32)


def kernel(x, mem, ln_in_g, ln_in_b, w_in, b_in, a_ln_g, a_ln_b, a_ws, a_bs, b_rel, t5_table,
           w_mem_kv, w_out, b_out, ln_g, ln_b):
    bsz, seq, _ = x.shape
    assert seq % MIX_TILE == 0 and x.shape[2] == D_MODEL
    m = bsz * seq
    nt = seq // TILE

    cpos = np.arange(A_BLOCK) // CHUNK
    a_mask = jnp.asarray(cpos[None, :] <= cpos[:, None])
    t5_rel = _t5_rel_table(t5_table)
    tri = jnp.asarray(np.tril(np.ones((TILE, TILE), np.float32), -1)).astype(BF16)

    h = x.reshape(m, D_MODEL)
    for l in range(DEPTH):
        sdt = BF16 if l == DEPTH - 1 else F32
        w_main, b_main, w_idx, b_idx, wvt, bvt = _prep_in_weights(w_in[l], b_in[l], sdt)
        proj = list(_in_projection(h, w_main, b_main, w_idx, b_idx, wvt, bvt, bsz, seq,
                                   input_norm=(ln_in_g, ln_in_b) if l == 0 else None))
        if l == 0:
            h = proj.pop()
        uv, bqkv, cqk, cvt, iq, ik, iw, mq, gates = proj

        ws_masked = jnp.where(a_mask[None], a_ws[l], 0.0).astype(sdt)
        bs_lanes = jnp.repeat(jnp.transpose(a_bs[l]), HEAD_DIM, axis=1)
        ya = _gmlp(uv, gates, a_ln_g[l].reshape(1, -1), a_ln_b[l].reshape(1, -1), ws_masked, bs_lanes)

        yb = _band_attention(bqkv.reshape(bsz, nt, TILE, 768), gates, _band_bias_table(b_rel[l]), bsz, seq)

        yc = _dsa_attention(cqk.reshape(bsz, nt, TILE, 512), cvt, ik.reshape(bsz, nt, TILE, 256),
                            iq, iw, gates, t5_rel, tri, bsz, seq)

        km, vm = _memory_kv(mem, w_mem_kv[l].astype(sdt))
        ym = _memory_attention(mq, km, vm, gates, bsz, seq)

        h = _out_projection(h, ya, yb, yc, ym, w_out[l].astype(sdt), b_out[l].reshape(1, -1),
                            ln_g[l].reshape(1, -1), ln_b[l].reshape(1, -1))
    return h.reshape(bsz, seq, D_MODEL)
```
